```python
import math
import jax, jax.numpy as jnp
from jax import lax
import numpy as np

D_MODEL = 2048
BATCH = 4
SEQ = 2048
DEPTH = 1
DEC_BATCH = 32
DEC_SEQ = 8
PAST_LEN = 16384
PAGE_SIZE = 128

HEAD_DIM = 128
N_HEADS = D_MODEL // HEAD_DIM
N_FOX = N_HEADS // 2
N_RET = N_HEADS - N_FOX
W_FOX = N_FOX * HEAD_DIM
W_RET = N_RET * HEAD_DIM
IN_COLS = 3 * W_FOX + N_FOX + 4 * W_RET
Q_BLOCK = 128
RET_CHUNK = 128
RET_K_SCALE = HEAD_DIM ** -0.5
ATTN_SCALE = HEAD_DIM ** -0.5
ROPE_BASE = 10000.0
PLE_DIM = 256
PEER_HEADS = 8
PEER_KEYS = 128
PEER_EXPERTS = PEER_KEYS * PEER_KEYS
PEER_QDIM = 256
PEER_HALF = PEER_QDIM // 2
PEER_TOPK = 16
PEER_TOKEN_BLOCK = 128
LN_EPS = 1e-5
GN_EPS = 1e-6
FORGET_BIAS_LO = 6.0
FORGET_BIAS_HI = 9.0
FORGET_COL_SCALE = 0.1
DEEPNORM_ALPHA = (2.0 * DEPTH) ** 0.25
DEEPNORM_BETA = (8.0 * DEPTH) ** -0.25

kernel_name = "hybrid_fox_retention_peer_step"


def layer_norm(x, g, b):
    xf = x.astype(jnp.float32)
    mu = jnp.mean(xf, -1, keepdims=True)
    var = jnp.mean(jnp.square(xf - mu), -1, keepdims=True)
    return ((xf - mu) * lax.rsqrt(var + LN_EPS) * g + b).astype(x.dtype)


def split_projections(h, w_in, b_f):
    B, T, _ = h.shape
    z = h @ w_in
    widths = (W_FOX, W_FOX, W_FOX, N_FOX, W_RET, W_RET, W_RET, W_RET)
    cuts = [int(c) for c in np.cumsum(widths)[:-1]]
    q_f, k_f, v_f, f_lin, q_r, k_r, v_r, g_r = jnp.split(z, cuts, axis=-1)
    logf = jax.nn.log_sigmoid((f_lin + b_f).astype(jnp.float32))
    hf = lambda a: a.reshape(B, T, N_FOX, HEAD_DIM)
    hr = lambda a: a.reshape(B, T, N_RET, HEAD_DIM)
    return hf(q_f), hf(k_f), hf(v_f), logf, hr(q_r), hr(k_r), hr(v_r), g_r


def rotary(x, pos):
    half = HEAD_DIM // 2
    inv_freq = ROPE_BASE ** (-jnp.arange(half, dtype=jnp.float32) / half)
    ang = pos[:, None] * inv_freq[None, :]
    cos = jnp.cos(ang)[None, :, None, :]
    sin = jnp.sin(ang)[None, :, None, :]
    x1 = x[..., :half].astype(jnp.float32)
    x2 = x[..., half:].astype(jnp.float32)
    return jnp.concatenate([x1 * cos - x2 * sin, x1 * sin + x2 * cos], -1).astype(x.dtype)


def fox_prompt(q, k, v, logf):
    B, S, H, d = q.shape
    c = jnp.cumsum(logf, axis=1)
    c_key = c.transpose(0, 2, 1)
    nblk = S // Q_BLOCK
    qb = q.reshape(B, nblk, Q_BLOCK, H, d).swapaxes(0, 1)
    cb = c.reshape(B, nblk, Q_BLOCK, H).swapaxes(0, 1)
    starts = jnp.arange(nblk) * Q_BLOCK
    kpos = jnp.arange(S)

    def block(args):
        qi, ci, s0 = args
        logits = jnp.einsum('bqhd,bkhd->bhqk', qi, k).astype(jnp.float32) * ATTN_SCALE
        logits = logits + ci.transpose(0, 2, 1)[:, :, :, None] - c_key[:, :, None, :]
        qpos = s0 + jnp.arange(Q_BLOCK)
        mask = kpos[None, :] <= qpos[:, None]
        logits = jnp.where(mask[None, None], logits, -jnp.inf)
        p = jax.nn.softmax(logits, axis=-1).astype(v.dtype)
        return jnp.einsum('bhqk,bkhd->bqhd', p, v)

    o = lax.map(block, (qb, cb, starts))
    return o.swapaxes(0, 1).reshape(B, S, H * d)


def fox_sample(q, k_new, v_new, logf_new, k_cache, v_cache, logf_cache, page_table):
    def one(args):
        pt, qs, ks, vs, ls = args
        kp = k_cache[pt].reshape(-1, N_FOX, HEAD_DIM)
        vp = v_cache[pt].reshape(-1, N_FOX, HEAD_DIM)
        lp = logf_cache[pt].reshape(-1, N_FOX).astype(jnp.float32)
        suffix = lax.cumsum(lp, axis=0, reverse=True) - lp
        cn = jnp.cumsum(ls, axis=0)
        k_all = jnp.concatenate([kp, ks.astype(kp.dtype)], 0)
        v_all = jnp.concatenate([vp, vs.astype(vp.dtype)], 0)
        c_key = jnp.concatenate([-suffix, cn], 0)
        T, P = qs.shape[0], kp.shape[0]
        logits = jnp.einsum('thd,shd->hts', qs, k_all).astype(jnp.float32) * ATTN_SCALE
        logits = logits + cn.T[:, :, None] - c_key.T[:, None, :]
        mask = jnp.concatenate([jnp.ones((T, P), bool), jnp.tril(jnp.ones((T, T), bool))], 1)
        logits = jnp.where(mask[None], logits, -jnp.inf)
        p = jax.nn.softmax(logits, axis=-1).astype(v_all.dtype)
        return jnp.einsum('hts,shd->thd', p, v_all).reshape(T, N_FOX * HEAD_DIM)

    return lax.map(one, (page_table, q, k_new, v_new, logf_new))


def retention_log_decay():
    return jnp.log1p(-jnp.exp2(-5.0 - jnp.arange(N_RET, dtype=jnp.float32)))


def retention_chunk(state, q, k, v, log_gamma):
    L = q.shape[1]
    dt = q.dtype
    pos = jnp.arange(L, dtype=jnp.float32)
    diff = pos[:, None] - pos[None, :]
    decay = jnp.where((diff >= 0)[None],
                      jnp.exp(jnp.maximum(diff, 0.0)[None] * log_gamma[:, None, None]), 0.0).astype(dt)
    scores = jnp.einsum('bthd,bshd->bhts', q, k) * decay[None]
    y_inner = jnp.einsum('bhts,bshe->bthe', scores, v)
    s_in = state.astype(dt)
    q_decay = jnp.exp((pos[:, None] + 1.0) * log_gamma[None, :]).astype(dt)
    y_cross = jnp.einsum('bthd,bhde->bthe', q, s_in) * q_decay[None, :, :, None]
    k_decay = jnp.exp((L - 1.0 - pos)[:, None] * log_gamma[None, :]).astype(dt)
    chunk_decay = jnp.exp(L * log_gamma).astype(dt)
    new_state = chunk_decay[None, :, None, None] * s_in + jnp.einsum('bshd,bshe,sh->bhde', k, v, k_decay)
    return new_state.astype(state.dtype), y_inner + y_cross


def retention_prompt(state0, q, k, v, log_gamma):
    B, S, H, d = q.shape
    nc = S // RET_CHUNK
    to_chunks = lambda a: a.reshape(B, nc, RET_CHUNK, H, d).swapaxes(0, 1)

    def step(s, inp):
        qc, kc, vc = inp
        return retention_chunk(s, qc, kc, vc, log_gamma)

    s_final, ys = lax.scan(step, state0, (to_chunks(q), to_chunks(k), to_chunks(v)))
    return s_final, ys.swapaxes(0, 1).reshape(B, S, H, d)


def retention_out(y, gate, gn_g):
    B, T, H, d = y.shape
    yf = y.astype(jnp.float32)
    mu = jnp.mean(yf, -1, keepdims=True)
    var = jnp.mean(jnp.square(yf - mu), -1, keepdims=True)
    yn = ((yf - mu) * lax.rsqrt(var + GN_EPS)).reshape(B, T, H * d).astype(y.dtype) * gn_g
    return jax.nn.silu(gate) * yn


def peer(h, w_pq, peer_keys, peer_u, peer_v):
    B, T, D = h.shape
    xt = h.reshape(-1, D)
    n = xt.shape[0]
    nb = -(-n // PEER_TOKEN_BLOCK)
    xt = jnp.pad(xt, ((0, nb * PEER_TOKEN_BLOCK - n), (0, 0))).reshape(nb, PEER_TOKEN_BLOCK, D)

    def block(xb):
        q = (xb @ w_pq).reshape(-1, PEER_HEADS, 2, PEER_HALF)
        s = jnp.einsum('thcd,hcnd->thcn', q, peer_keys).astype(jnp.float32)
        sv, si = lax.top_k(s, PEER_TOPK)
        cand = sv[:, :, 0, :, None] + sv[:, :, 1, None, :]
        cv, ci = lax.top_k(cand.reshape(cand.shape[0], PEER_HEADS, -1), PEER_TOPK)
        i1 = jnp.take_along_axis(si[:, :, 0], ci // PEER_TOPK, axis=-1)
        i2 = jnp.take_along_axis(si[:, :, 1], ci % PEER_TOPK, axis=-1)
        idx = i1 * PEER_KEYS + i2
        g = jax.nn.softmax(cv, axis=-1)
        a = jnp.einsum('thkd,td->thk', peer_u[idx], xb)
        w = (jax.nn.gelu(a.astype(jnp.float32), approximate=False) * g).astype(xb.dtype)
        return jnp.einsum('thk,thkd->td', w, peer_v[idx])

    out = lax.map(block, xt).reshape(-1, D)[:n]
    return out.reshape(B, T, D)


def post_mixer(x, mixed, p_emb, w_o, ln1_g, ln1_b, w_pq, peer_keys, peer_u, peer_v,
               ln2_g, ln2_b, w_pg, b_pg, w_pe):
    h = layer_norm(DEEPNORM_ALPHA * x + mixed @ w_o, ln1_g, ln1_b)
    h = layer_norm(DEEPNORM_ALPHA * h + peer(h, w_pq, peer_keys, peer_u, peer_v), ln2_g, ln2_b)
    gate = jax.nn.sigmoid(h @ w_pg + b_pg)
    return h + gate * (p_emb @ w_pe)


def setup_inputs(seed: int = 0) -> dict:
    key = jax.random.key(seed)
    ks = jax.random.split(key, 26)
    f32 = jnp.float32
    nrm = lambda k, shape, s: jax.random.normal(k, shape, f32) * s
    n_pages = PAST_LEN // PAGE_SIZE
    n_used = DEC_BATCH * n_pages
    n_pool = n_used + (n_used + 3) // 4
    col_scale = np.ones((IN_COLS,), np.float32)
    col_scale[2 * W_FOX:3 * W_FOX] = DEEPNORM_BETA
    col_scale[3 * W_FOX:3 * W_FOX + N_FOX] = FORGET_COL_SCALE
    v_ret0 = 3 * W_FOX + N_FOX + 2 * W_RET
    col_scale[v_ret0:v_ret0 + W_RET] = DEEPNORM_BETA
    return {
        "x_prompt": nrm(ks[0], (BATCH, SEQ, D_MODEL), 1.0),
        "x_sample": nrm(ks[1], (DEC_BATCH, DEC_SEQ, D_MODEL), 1.0),
        "cache_k": nrm(ks[2], (DEPTH, n_pool, PAGE_SIZE, N_FOX, HEAD_DIM), 1.0),
        "cache_v": nrm(ks[3], (DEPTH, n_pool, PAGE_SIZE, N_FOX, HEAD_DIM), 1.0),
        "cache_logf": jax.nn.log_sigmoid(jax.random.uniform(
            ks[4], (DEPTH, n_pool, PAGE_SIZE, N_FOX), f32, FORGET_BIAS_LO, FORGET_BIAS_HI)),
        "state_ret": nrm(ks[5], (DEPTH, DEC_BATCH, N_RET, HEAD_DIM, HEAD_DIM), 0.1),
        "page_table": jax.random.permutation(ks[6], n_pool)[:n_used].reshape(DEC_BATCH, n_pages).astype(jnp.int32),
        "p_prompt": nrm(ks[7], (DEPTH, BATCH, SEQ, PLE_DIM), 1.0),
        "p_sample": nrm(ks[8], (DEPTH, DEC_BATCH, DEC_SEQ, PLE_DIM), 1.0),
        "w_in": nrm(ks[9], (DEPTH, D_MODEL, IN_COLS), D_MODEL ** -0.5) * jnp.asarray(col_scale),
        "b_f": jax.random.uniform(ks[10], (DEPTH, N_FOX), f32, FORGET_BIAS_LO, FORGET_BIAS_HI),
        "gn_g": 1.0 + nrm(ks[11], (DEPTH, W_RET), 0.02),
        "w_o": nrm(ks[12], (DEPTH, W_FOX + W_RET, D_MODEL), (W_FOX + W_RET) ** -0.5 * DEEPNORM_BETA),
        "ln1_g": 1.0 + nrm(ks[13], (DEPTH, D_MODEL), 0.02),
        "ln1_b": nrm(ks[14], (DEPTH, D_MODEL), 0.02),
        "w_pq": nrm(ks[15], (DEPTH, D_MODEL, PEER_HEADS * PEER_QDIM), D_MODEL ** -0.5),
        "peer_keys": nrm(ks[16], (DEPTH, PEER_HEADS, 2, PEER_KEYS, PEER_HALF), PEER_HALF ** -0.5),
        "peer_u": nrm(ks[17], (DEPTH, PEER_EXPERTS, D_MODEL), D_MODEL ** -0.5 * DEEPNORM_BETA),
        "peer_v": nrm(ks[18], (DEPTH, PEER_EXPERTS, D_MODEL), DEEPNORM_BETA),
        "ln2_g": 1.0 + nrm(ks[19], (DEPTH, D_MODEL), 0.02),
        "ln2_b": nrm(ks[20], (DEPTH, D_MODEL), 0.02),
        "w_pg": nrm(ks[21], (DEPTH, D_MODEL, D_MODEL), D_MODEL ** -0.5),
        "b_pg": nrm(ks[22], (DEPTH, D_MODEL), 0.02),
        "w_pe": nrm(ks[23], (DEPTH, PLE_DIM, D_MODEL), PLE_DIM ** -0.5),
    }


def reference(x_prompt, x_sample, cache_k, cache_v, cache_logf, state_ret, page_table,
              p_prompt, p_sample, w_in, b_f, gn_g, w_o, ln1_g, ln1_b, w_pq, peer_keys,
              peer_u, peer_v, ln2_g, ln2_b, w_pg, b_pg, w_pe):
    log_gamma = retention_log_decay()
    pos_p = jnp.arange(SEQ, dtype=jnp.float32)
    pos_s = PAST_LEN + jnp.arange(DEC_SEQ, dtype=jnp.float32)
    xp, xs = x_prompt, x_sample
    kp_l, vp_l, lp_l, rp_l = [], [], [], []
    ks_l, vs_l, ls_l, rs_l = [], [], [], []
    for i in range(DEPTH):
        tail = (w_o[i], ln1_g[i], ln1_b[i], w_pq[i], peer_keys[i], peer_u[i], peer_v[i],
                ln2_g[i], ln2_b[i], w_pg[i], b_pg[i], w_pe[i])
        q_f, k_f, v_f, lf, q_r, k_r, v_r, g_r = split_projections(xp, w_in[i], b_f[i])
        o_fox = fox_prompt(q_f, k_f, v_f, lf)
        q_r = rotary(q_r, pos_p)
        k_r = rotary(k_r, pos_p) * RET_K_SCALE
        s0 = jnp.zeros((xp.shape[0], N_RET, HEAD_DIM, HEAD_DIM), q_r.dtype)
        s_p, y_r = retention_prompt(s0, q_r, k_r, v_r, log_gamma)
        o_ret = retention_out(y_r, g_r, gn_g[i])
        xp_next = post_mixer(xp, jnp.concatenate([o_fox, o_ret], -1), p_prompt[i], *tail)
        kp_l.append(k_f); vp_l.append(v_f); lp_l.append(lf); rp_l.append(s_p)
        q_f, k_f, v_f, lf, q_r, k_r, v_r, g_r = split_projections(xs, w_in[i], b_f[i])
        o_fox = fox_sample(q_f, k_f, v_f, lf, cache_k[i], cache_v[i], cache_logf[i], page_table)
        q_r = rotary(q_r, pos_s)
        k_r = rotary(k_r, pos_s) * RET_K_SCALE
        s_s, y_r = retention_chunk(state_ret[i], q_r, k_r, v_r, log_gamma)
        o_ret = retention_out(y_r, g_r, gn_g[i])
        xs_next = post_mixer(xs, jnp.concatenate([o_fox, o_ret], -1), p_sample[i], *tail)
        ks_l.append(k_f); vs_l.append(v_f); ls_l.append(lf); rs_l.append(s_s)
        xp, xs = xp_next, xs_next
    return (xp, xs, jnp.stack(kp_l), jnp.stack(vp_l), jnp.stack(lp_l), jnp.stack(rp_l),
            jnp.stack(ks_l), jnp.stack(vs_l), jnp.stack(ls_l), jnp.stack(rs_l))
```

```python
import functools
import math

import numpy as np
import jax
import jax.numpy as jnp
from jax import lax
from jax.experimental import pallas as pl
from jax.experimental.pallas import tpu as pltpu

F32 = jnp.float32
BF16 = jnp.bfloat16

HEAD_DIM = 128
PAGE_SIZE = 128
ROPE_BASE = 10000.0
RET_CHUNK = 128
PEER_HEADS = 8
PEER_KEYS = 128
PEER_HALF = 128
PEER_TOPK = 16
LN_EPS = 1e-5
GN_EPS = 1e-6
NEG_INF = float("-inf")
BIG = 1e30

VMEM_LIMIT = 56 * 1024 * 1024

_NT = (((1,), (1,)), ((), ()))
_TN = (((0,), (0,)), ((), ()))


def _pick(n, candidates):
    for c in candidates:
        if n % c == 0:
            return c
    raise ValueError(f"no tile in {candidates} divides {n}")


def _idiv(x, c):
    assert c & (c - 1) == 0
    return lax.shift_right_logical(x, int(math.log2(c)))


def _imod(x, c):
    assert c & (c - 1) == 0
    return x & (c - 1)


def _params(*sem):
    return pltpu.CompilerParams(dimension_semantics=sem, vmem_limit_bytes=VMEM_LIMIT)


def _mm_body(x_ref, w_ref, o_ref):
    o_ref[...] = jnp.dot(x_ref[...], w_ref[...], preferred_element_type=F32).astype(o_ref.dtype)


def _matmul(x, w, tm, tn, out_dtype, name):
    m, k = x.shape
    n = w.shape[1]
    return pl.pallas_call(
        _mm_body,
        grid=(m // tm, n // tn),
        in_specs=[pl.BlockSpec((tm, k), lambda i, j: (i, 0)), pl.BlockSpec((k, tn), lambda i, j: (0, j))],
        out_specs=pl.BlockSpec((tm, tn), lambda i, j: (i, j)),
        out_shape=jax.ShapeDtypeStruct((m, n), out_dtype),
        compiler_params=_params("parallel", "parallel"),
        name=name,
    )(x, w)


def _logf_body(wf_ref, x_ref, b_ref, o_ref):
    f = lax.dot_general(wf_ref[...], x_ref[...], _NT, preferred_element_type=F32) + b_ref[...]
    o_ref[...] = jnp.minimum(f, 0.0) - jnp.log1p(jnp.exp(-jnp.abs(f)))


def _log_forget_t(wf_t, xb, b_col, tm):
    h, k = wf_t.shape
    n = xb.shape[0]
    return pl.pallas_call(
        _logf_body,
        grid=(n // tm,),
        in_specs=[pl.BlockSpec((h, k), lambda i: (0, 0)), pl.BlockSpec((tm, k), lambda i: (i, 0)),
                  pl.BlockSpec((h, 1), lambda i: (0, 0))],
        out_specs=pl.BlockSpec((h, tm), lambda i: (0, i)),
        out_shape=jax.ShapeDtypeStruct((h, n), F32),
        compiler_params=_params("parallel"),
        name="log_forget",
    )(wf_t, xb, b_col)


def _split3(x):
    hi = x.astype(BF16)
    r1 = x - hi.astype(F32)
    mid = r1.astype(BF16)
    lo = (r1 - mid.astype(F32)).astype(BF16)
    return hi, mid, lo


def _dot3(parts, m):
    acc = jnp.dot(parts[0], m, preferred_element_type=F32)
    acc = acc + jnp.dot(parts[1], m, preferred_element_type=F32)
    return acc + jnp.dot(parts[2], m, preferred_element_type=F32)


def _cumsum_body(x_ref, o_ref, *, seg, blk):
    rows, length = x_ref.shape
    r = lax.broadcasted_iota(jnp.int32, (blk, blk), 0)
    c = lax.broadcasted_iota(jnp.int32, (blk, blk), 1)
    if seg >= blk:
        tri = (r <= c)
    else:
        tri = (r <= c) & (_idiv(r, seg) == _idiv(c, seg))
    tri = jnp.where(tri, 1.0, 0.0).astype(BF16)
    carry = jnp.zeros((rows, 1), F32)
    for j in range(length // blk):
        if seg < blk or (j * blk) % seg == 0:
            carry = jnp.zeros((rows, 1), F32)
        cs = _dot3(_split3(x_ref[:, j * blk:(j + 1) * blk]), tri) + carry
        o_ref[:, j * blk:(j + 1) * blk] = cs
        carry = cs[:, blk - 1:blk]


def _segment_cumsum(x, seg):
    rows, length = x.shape
    blk = _pick(length, (256, 128))
    assert seg % blk == 0 or blk % seg == 0
    return pl.pallas_call(
        functools.partial(_cumsum_body, seg=seg, blk=blk),
        out_shape=jax.ShapeDtypeStruct((rows, length), F32),
        compiler_params=pltpu.CompilerParams(vmem_limit_bytes=VMEM_LIMIT),
        name="segment_cumsum",
    )(x)


def _fox_prompt_body(q_ref, k_ref, v_ref, c_ref, o_ref, *, tq, scale):
    s_len = q_ref.shape[0]
    row = lax.broadcasted_iota(jnp.int32, (tq, tq), 0)
    col = lax.broadcasted_iota(jnp.int32, (tq, tq), 1)
    causal = col <= row

    def step(q, kj, carry, mask):
        m, l, acc = carry
        ks = kj * tq if isinstance(kj, int) else pl.multiple_of(kj * tq, tq)
        k = k_ref[pl.ds(ks, tq), :].astype(BF16)
        v = v_ref[pl.ds(ks, tq), :].astype(BF16)
        s = lax.dot_general(q, k, _NT, preferred_element_type=F32) - c_ref[kj]
        if mask:
            s = jnp.where(causal, s, NEG_INF)
        m_new = jnp.maximum(m, jnp.max(s, axis=1, keepdims=True))
        p = jnp.exp(s - m_new)
        alpha = jnp.exp(m - m_new)
        l = alpha * l + jnp.sum(p, axis=1, keepdims=True)
        acc = alpha * acc + jnp.dot(p.astype(BF16), v, preferred_element_type=F32)
        return m_new, l, acc

    for qi in range(s_len // tq):
        q = (q_ref[qi * tq:(qi + 1) * tq, :] * scale).astype(BF16)
        carry = (jnp.full((tq, 1), NEG_INF, F32), jnp.zeros((tq, 1), F32), jnp.zeros((tq, HEAD_DIM), F32))
        if qi > 0:
            carry = lax.fori_loop(0, qi, lambda kj, cr: step(q, kj, cr, False), carry)
        m, l, acc = step(q, qi, carry, True)
        o_ref[qi * tq:(qi + 1) * tq, :] = (acc / l).astype(o_ref.dtype)


def _fox_prompt(z, c_t, batch, seq, n_fox):
    tq = _pick(seq, (256, 128))
    c4 = c_t.reshape(n_fox, batch * seq // tq, 1, tq)
    nb = seq // tq
    return pl.pallas_call(
        functools.partial(_fox_prompt_body, tq=tq, scale=HEAD_DIM ** -0.5),
        grid=(batch, n_fox),
        in_specs=[
            pl.BlockSpec((seq, HEAD_DIM), lambda b, h: (b, h)),
            pl.BlockSpec((seq, HEAD_DIM), lambda b, h: (b, n_fox + h)),
            pl.BlockSpec((seq, HEAD_DIM), lambda b, h: (b, 2 * n_fox + h)),
            pl.BlockSpec((None, nb, 1, tq), lambda b, h: (h, b, 0, 0)),
        ],
        out_specs=pl.BlockSpec((seq, HEAD_DIM), lambda b, h: (b, h)),
        out_shape=jax.ShapeDtypeStruct((batch * seq, n_fox * HEAD_DIM), BF16),
        compiler_params=_params("parallel", "parallel"),
        name="fox_prompt",
    )(z, z, z, c4)


def _page_suffix_body(lp_ref, e_ref, t_ref):
    n = lp_ref.shape[1]
    n_h = n // PAGE_SIZE
    r = lax.broadcasted_iota(jnp.int32, (n, n), 0)
    c = lax.broadcasted_iota(jnp.int32, (n, n), 1)
    same_head = _imod(r, n_h) == _idiv(c, PAGE_SIZE)
    later = _idiv(r, n_h) > _imod(c, PAGE_SIZE)
    m_e = jnp.where(same_head & later, 1.0, 0.0).astype(BF16)
    m_t = jnp.where(same_head, 1.0, 0.0).astype(BF16)
    parts = _split3(lp_ref[...])
    e_ref[...] = _dot3(parts, m_e)
    t_ref[...] = _dot3(parts, m_t)


def _page_suffix(lp_rows):
    n_pool, n = lp_rows.shape
    tr = _pick(n_pool, (512, 256, 128, 64, 32, 16, 8))
    spec = pl.BlockSpec((tr, n), lambda i: (i, 0))
    return pl.pallas_call(
        _page_suffix_body,
        grid=(n_pool // tr,),
        in_specs=[spec],
        out_specs=[spec, spec],
        out_shape=[jax.ShapeDtypeStruct((n_pool, n), F32)] * 2,
        compiler_params=_params("parallel"),
        name="page_suffix",
    )(lp_rows)


def _rows_per_head(x, t):
    h, l = x.shape
    return jnp.broadcast_to(x[:, None, :], (h, t, l)).reshape(h * t, l)


def _fox_sample_body(pt_ref, q_ref, kn_ref, vn_ref, cn_ref, kc_ref, vc_ref, e_ref, t_ref, o_ref,
                     qbd, m_sc, l_sc, acc, carry, *, scale, n_fox):
    p = pl.program_id(1)
    t = q_ref.shape[0]
    rows = n_fox * t
    width = n_fox * HEAD_DIM

    def update(s, v_bf):
        m_old = m_sc[...]
        m_new = jnp.maximum(m_old, jnp.max(s, axis=1, keepdims=True))
        pr = jnp.exp(s - m_new)
        alpha = jnp.exp(m_old - m_new)
        l_sc[...] = alpha * l_sc[...] + jnp.sum(pr, axis=1, keepdims=True)
        acc[...] = alpha * acc[...] + jnp.dot(pr.astype(BF16), v_bf, preferred_element_type=F32)
        m_sc[...] = m_new

    @pl.when(p == 0)
    def _init():
        r = lax.broadcasted_iota(jnp.int32, (rows, width), 0)
        c = lax.broadcasted_iota(jnp.int32, (rows, width), 1)
        q_rep = jnp.concatenate([q_ref[...]] * n_fox, axis=0) * scale
        qbd[...] = jnp.where(_idiv(r, t) == _idiv(c, HEAD_DIM), q_rep, 0.0).astype(BF16)
        m_sc[...] = jnp.full(m_sc.shape, NEG_INF, F32)
        l_sc[...] = jnp.zeros(l_sc.shape, F32)
        acc[...] = jnp.zeros(acc.shape, F32)
        carry[...] = jnp.zeros(carry.shape, F32)
        pad = jnp.zeros((PAGE_SIZE - t, width), BF16)
        k_new = jnp.concatenate([kn_ref[...].astype(BF16), pad], axis=0)
        v_new = jnp.concatenate([vn_ref[...].astype(BF16), pad], axis=0)
        s = lax.dot_general(qbd[...], k_new, _NT, preferred_element_type=F32) - _rows_per_head(cn_ref[...], t)
        rr = lax.broadcasted_iota(jnp.int32, (rows, PAGE_SIZE), 0)
        cc = lax.broadcasted_iota(jnp.int32, (rows, PAGE_SIZE), 1)
        s = jnp.where(cc <= _imod(rr, t), s, NEG_INF)
        update(s, v_new)

    bias = e_ref[...] + carry[...]
    s = lax.dot_general(qbd[...], kc_ref[...].astype(BF16), _NT, preferred_element_type=F32)
    update(s + _rows_per_head(bias, t), vc_ref[...].astype(BF16))
    carry[...] = carry[...] + t_ref[...]

    @pl.when(p == pl.num_programs(1) - 1)
    def _fin():
        o_all = acc[...] / l_sc[...]
        for h in range(n_fox):
            o_ref[:, h * HEAD_DIM:(h + 1) * HEAD_DIM] = o_all[
                h * t:(h + 1) * t, h * HEAD_DIM:(h + 1) * HEAD_DIM].astype(o_ref.dtype)


def _fox_sample(z, row0, cn_pad, cache_k, cache_v, suffix, totals, page_table, t, n_fox):
    db, n_pages = page_table.shape
    width = n_fox * HEAD_DIM
    rb0 = row0 // t
    last = n_pages - 1
    page = lambda b, p, pt: (pt[b, last - p], 0, 0)
    grid_spec = pltpu.PrefetchScalarGridSpec(
        num_scalar_prefetch=1,
        grid=(db, n_pages),
        in_specs=[
            pl.BlockSpec((t, width), lambda b, p, pt: (rb0 + b, 0)),
            pl.BlockSpec((t, width), lambda b, p, pt: (rb0 + b, 1)),
            pl.BlockSpec((t, width), lambda b, p, pt: (rb0 + b, 2)),
            pl.BlockSpec((None, n_fox, PAGE_SIZE), lambda b, p, pt: (b, 0, 0)),
            pl.BlockSpec((None, PAGE_SIZE, width), page),
            pl.BlockSpec((None, PAGE_SIZE, width), page),
            pl.BlockSpec((None, n_fox, PAGE_SIZE), page),
            pl.BlockSpec((None, n_fox, PAGE_SIZE), page),
        ],
        out_specs=pl.BlockSpec((t, width), lambda b, p, pt: (b, 0)),
        scratch_shapes=[
            pltpu.VMEM((n_fox * t, width), BF16),
            pltpu.VMEM((n_fox * t, 1), F32),
            pltpu.VMEM((n_fox * t, 1), F32),
            pltpu.VMEM((n_fox * t, width), F32),
            pltpu.VMEM((n_fox, PAGE_SIZE), F32),
        ],
    )
    return pl.pallas_call(
        functools.partial(_fox_sample_body, scale=HEAD_DIM ** -0.5, n_fox=n_fox),
        grid_spec=grid_spec,
        out_shape=jax.ShapeDtypeStruct((db * t, width), F32),
        compiler_params=_params("parallel", "arbitrary"),
        name="fox_sample",
    )(page_table, z, z, z, cn_pad, cache_k, cache_v, suffix, totals)


def _rotate(x, cos_f, sin_f):
    return x * cos_f + pltpu.roll(x, HEAD_DIM // 2, 1) * sin_f


def _retention_body(q_ref, k_ref, v_ref, g_ref, cos_ref, sin_ref, dec_ref, qd_ref, kd_ref, cd_ref, gn_ref,
                    s0_ref, o_ref, sout_ref, state, *, k_scale):
    c = pl.program_id(2)

    @pl.when(c == 0)
    def _():
        state[...] = s0_ref[...]

    cos_f = cos_ref[...]
    sin_f = sin_ref[...]
    q = _rotate(q_ref[...], cos_f, sin_f)
    k = _rotate(k_ref[...], cos_f, sin_f) * k_scale
    qb = q.astype(BF16)
    vb = v_ref[...].astype(BF16)
    s_in = state[...]
    scores = lax.dot_general(qb, k.astype(BF16), _NT, preferred_element_type=F32) * dec_ref[...]
    y = jnp.dot(scores.astype(BF16), vb, preferred_element_type=F32)
    y = y + jnp.dot(qb, s_in.astype(BF16), preferred_element_type=F32) * qd_ref[...]
    kd = (k * kd_ref[...]).astype(BF16)
    new_state = cd_ref[...] * s_in + lax.dot_general(kd, vb, _TN, preferred_element_type=F32)
    state[...] = new_state
    mu = jnp.mean(y, axis=1, keepdims=True)
    yc = y - mu
    var = jnp.mean(yc * yc, axis=1, keepdims=True)
    yn = yc * lax.rsqrt(var + GN_EPS) * gn_ref[...]
    g = g_ref[...]
    o_ref[...] = (g / (1.0 + jnp.exp(-g)) * yn).astype(o_ref.dtype)

    @pl.when(c == pl.num_programs(2) - 1)
    def _():
        sout_ref[...] = new_state


def _retention(z, row0, col0, n_ret, batch, seq, chunk, state0, cos_f, sin_f, tables, gn_row):
    dec, qd, kd, cd = tables
    nc = seq // chunk
    rb0 = row0 // chunk
    zspec = lambda off: pl.BlockSpec((chunk, HEAD_DIM), lambda b, h, c: (rb0 + b * nc + c, col0 + off * n_ret + h))
    d = HEAD_DIM
    out, s_fin = pl.pallas_call(
        functools.partial(_retention_body, k_scale=HEAD_DIM ** -0.5),
        grid=(batch, n_ret, nc),
        in_specs=[
            zspec(0), zspec(1), zspec(2), zspec(3),
            pl.BlockSpec((chunk, d), lambda b, h, c: (c, 0)),
            pl.BlockSpec((chunk, d), lambda b, h, c: (c, 0)),
            pl.BlockSpec((None, chunk, chunk), lambda b, h, c: (h, 0, 0)),
            pl.BlockSpec((None, chunk, d), lambda b, h, c: (h, 0, 0)),
            pl.BlockSpec((None, chunk, d), lambda b, h, c: (h, 0, 0)),
            pl.BlockSpec((None, 1, d), lambda b, h, c: (h, 0, 0)),
            pl.BlockSpec((1, d), lambda b, h, c: (0, h)),
            pl.BlockSpec((None, None, d, d), lambda b, h, c: (b, h, 0, 0)),
        ],
        out_specs=[
            pl.BlockSpec((chunk, d), lambda b, h, c: (b * nc + c, h)),
            pl.BlockSpec((None, None, d, d), lambda b, h, c: (b, h, 0, 0)),
        ],
        out_shape=[
            jax.ShapeDtypeStruct((batch * seq, n_ret * d), BF16 if chunk % 16 == 0 else F32),
            jax.ShapeDtypeStruct((batch, n_ret, d, d), F32),
        ],
        scratch_shapes=[pltpu.VMEM((d, d), F32)],
        compiler_params=_params("parallel", "parallel", "arbitrary"),
        name=f"retention_{chunk}",
    )(z, z, z, z, cos_f, sin_f, dec, qd, kd, cd, gn_row, state0)
    return out, s_fin


def _retention_tables(n_ret, chunk):
    log_gamma = jnp.log1p(-jnp.exp2(-5.0 - jnp.arange(n_ret, dtype=F32)))
    pos = jnp.arange(chunk, dtype=F32)
    diff = pos[:, None] - pos[None, :]
    dec = jnp.where((diff >= 0)[None], jnp.exp(jnp.maximum(diff, 0.0)[None] * log_gamma[:, None, None]), 0.0)
    qd = jnp.exp((pos[None, :] + 1.0) * log_gamma[:, None])
    kd = jnp.exp((chunk - 1.0 - pos)[None, :] * log_gamma[:, None])
    cd = jnp.exp(chunk * log_gamma)
    bc = lambda a: jnp.broadcast_to(a[:, :, None], (n_ret, chunk, HEAD_DIM))
    return dec, bc(qd), bc(kd), jnp.broadcast_to(cd[:, None, None], (n_ret, 1, HEAD_DIM))


def _rotary_tables(pos):
    half = HEAD_DIM // 2
    inv_freq = ROPE_BASE ** (-jnp.arange(half, dtype=F32) / half)
    ang = pos[:, None] * inv_freq[None, :]
    cos, sin = jnp.cos(ang), jnp.sin(ang)
    return jnp.concatenate([cos, cos], -1), jnp.concatenate([-sin, sin], -1)


def _layer_norm(x, g, b):
    mu = jnp.mean(x, axis=1, keepdims=True)
    xc = x - mu
    var = jnp.mean(xc * xc, axis=1, keepdims=True)
    return xc * lax.rsqrt(var + LN_EPS) * g + b


def _wo_ln_body(x_ref, of_ref, or_ref, w1_ref, w2_ref, g_ref, b_ref, h_ref, hb_ref, *, alpha):
    mixed = jnp.dot(of_ref[...], w1_ref[...], preferred_element_type=F32)
    mixed = mixed + jnp.dot(or_ref[...], w2_ref[...], preferred_element_type=F32)
    h = _layer_norm(alpha * x_ref[...] + mixed, g_ref[...], b_ref[...])
    h_ref[...] = h
    hb_ref[...] = h.astype(BF16)


def _wo_ln(x, o_fox, o_ret, w1, w2, g, b, alpha, tm):
    n, d = x.shape
    row = lambda w: pl.BlockSpec((tm, w), lambda i: (i, 0))
    full = lambda a: pl.BlockSpec(a.shape, lambda i: (0, 0))
    return pl.pallas_call(
        functools.partial(_wo_ln_body, alpha=alpha),
        grid=(n // tm,),
        in_specs=[row(d), row(o_fox.shape[1]), row(o_ret.shape[1]), full(w1), full(w2), full(g), full(b)],
        out_specs=[row(d), row(d)],
        out_shape=[jax.ShapeDtypeStruct((n, d), F32), jax.ShapeDtypeStruct((n, d), BF16)],
        compiler_params=_params("parallel"),
        name="wo_ln1",
    )(x, o_fox, o_ret, w1, w2, g, b)


def _top_desc(s, count):
    n = s.shape[0]
    iota = lax.broadcasted_iota(jnp.int32, s.shape, 0)
    vals = []
    work = s
    for _ in range(count):
        mj = jnp.max(work, axis=0, keepdims=True)
        vals.append(mj)
        first = jnp.min(jnp.where(work == mj, iota, n), axis=0, keepdims=True)
        work = jnp.where(iota == first, NEG_INF, work)
    return vals


_CAND = [(a, b) for a in range(PEER_TOPK + 1) for b in range(PEER_TOPK + 1) if (a + 1) * (b + 1) <= PEER_TOPK + 1]


def _peer_score_body(hb_ref, wq_ref, keys_ref, a1_ref, a2_ref, th_ref):
    q = jnp.dot(hb_ref[...], wq_ref[...], preferred_element_type=F32).astype(BF16)
    tm = q.shape[0]
    th_rows = []
    for h in range(PEER_HEADS):
        s = []
        for c in range(2):
            col = (2 * h + c) * PEER_HALF
            s.append(lax.dot_general(keys_ref[h, c], q[:, col:col + PEER_HALF], _NT, preferred_element_type=F32))
        v1 = _top_desc(s[0], PEER_TOPK + 1)
        v2 = _top_desc(s[1], PEER_TOPK + 1)
        rows = [v1[a] + v2[b] for a, b in _CAND]
        pad = (-len(rows)) % 8
        cand = jnp.concatenate(rows + [jnp.full((pad, tm), NEG_INF, F32)], axis=0)
        top = _top_desc(cand, PEER_TOPK + 1)
        z = jnp.ones((1, tm), F32)
        for j in range(1, PEER_TOPK):
            z = z + jnp.exp(top[j] - top[0])
        shift = top[0] + jnp.log(z)
        theta = 0.5 * (top[PEER_TOPK - 1] + top[PEER_TOPK])
        a1_ref[h] = s[0] - shift
        a2_ref[h] = s[1]
        th_rows.append(theta - shift)
    th_ref[...] = jnp.concatenate(th_rows, axis=0)


def _peer_scores(hb, wq, keys, tm):
    n, d = hb.shape
    return pl.pallas_call(
        _peer_score_body,
        grid=(n // tm,),
        in_specs=[pl.BlockSpec((tm, d), lambda i: (i, 0)), pl.BlockSpec(wq.shape, lambda i: (0, 0)),
                  pl.BlockSpec(keys.shape, lambda i: (0, 0, 0, 0))],
        out_specs=[pl.BlockSpec((PEER_HEADS, PEER_KEYS, tm), lambda i: (0, 0, i)),
                   pl.BlockSpec((PEER_HEADS, PEER_KEYS, tm), lambda i: (0, 0, i)),
                   pl.BlockSpec((PEER_HEADS, tm), lambda i: (0, i))],
        out_shape=[jax.ShapeDtypeStruct((PEER_HEADS, PEER_KEYS, n), F32),
                   jax.ShapeDtypeStruct((PEER_HEADS, PEER_KEYS, n), F32),
                   jax.ShapeDtypeStruct((PEER_HEADS, n), F32)],
        compiler_params=_params("parallel"),
        name="peer_scores",
    )(hb, wq, keys)


def _peer_dense_body(hb_ref, u_ref, v_ref, a1_ref, a2_ref, th_ref, o_ref, wt_sc, *, eb):
    j = pl.program_id(1)

    @pl.when(j == 0)
    def _():
        o_ref[...] = jnp.zeros(o_ref.shape, F32)

    act = lax.dot_general(u_ref[...], hb_ref[...], _NT, preferred_element_type=F32)
    act = 0.5 * act * (1.0 + lax.erf(act * (2.0 ** -0.5)))
    for r in range(eb // PEER_KEYS):
        i1 = j * (eb // PEER_KEYS) + r
        gate = None
        for h in range(PEER_HEADS):
            s = a1_ref[h, pl.ds(i1, 1), :] + a2_ref[h]
            g = jnp.where(s >= th_ref[h:h + 1, :], jnp.exp(s), 0.0)
            gate = g if gate is None else gate + g
        wt_sc[r * PEER_KEYS:(r + 1) * PEER_KEYS, :] = (act[r * PEER_KEYS:(r + 1) * PEER_KEYS, :] * gate).astype(BF16)
    o_ref[...] += lax.dot_general(wt_sc[...], v_ref[...], _TN, preferred_element_type=F32)


def _peer_dense(hb, u, v, a1, a2, th, tm, eb):
    n, d = hb.shape
    e = u.shape[0]
    return pl.pallas_call(
        functools.partial(_peer_dense_body, eb=eb),
        grid=(n // tm, e // eb),
        in_specs=[
            pl.BlockSpec((tm, d), lambda i, j: (i, 0)),
            pl.BlockSpec((eb, d), lambda i, j: (j, 0)),
            pl.BlockSpec((eb, d), lambda i, j: (j, 0)),
            pl.BlockSpec((PEER_HEADS, PEER_KEYS, tm), lambda i, j: (0, 0, i)),
            pl.BlockSpec((PEER_HEADS, PEER_KEYS, tm), lambda i, j: (0, 0, i)),
            pl.BlockSpec((PEER_HEADS, tm), lambda i, j: (0, i)),
        ],
        out_specs=pl.BlockSpec((tm, d), lambda i, j: (i, 0)),
        out_shape=jax.ShapeDtypeStruct((n, d), F32),
        scratch_shapes=[pltpu.VMEM((eb, tm), BF16)],
        compiler_params=_params("parallel", "arbitrary"),
        name="peer_dense",
    )(hb, u, v, a1, a2, th)


def _out_body(h_ref, po_ref, p_ref, g_ref, b_ref, wg_ref, bg_ref, we_ref, y_ref, *, alpha):
    h2 = _layer_norm(alpha * h_ref[...] + po_ref[...], g_ref[...], b_ref[...])
    lin = jnp.dot(h2.astype(BF16), wg_ref[...], preferred_element_type=F32) + bg_ref[...]
    emb = jnp.dot(p_ref[...].astype(BF16), we_ref[...], preferred_element_type=F32)
    y_ref[...] = h2 + emb / (1.0 + jnp.exp(-lin))


def _out_stage(h1, peer_out, p_emb, g, b, wg, bg, we, alpha, tm):
    n, d = h1.shape
    row = lambda w: pl.BlockSpec((tm, w), lambda i: (i, 0))
    full = lambda a: pl.BlockSpec(a.shape, lambda i: (0, 0))
    return pl.pallas_call(
        functools.partial(_out_body, alpha=alpha),
        grid=(n // tm,),
        in_specs=[row(d), row(d), row(p_emb.shape[1]), full(g), full(b), full(wg), full(bg), full(we)],
        out_specs=row(d),
        out_shape=jax.ShapeDtypeStruct((n, d), F32),
        compiler_params=_params("parallel"),
        name="ln2_gate_out",
    )(h1, peer_out, p_emb, g, b, wg, bg, we)


def kernel(x_prompt, x_sample, cache_k, cache_v, cache_logf, state_ret, page_table, p_prompt, p_sample, w_in, b_f,
           gn_g, w_o, ln1_g, ln1_b, w_pq, peer_keys, peer_u, peer_v, ln2_g, ln2_b, w_pg, b_pg, w_pe):
    depth = w_in.shape[0]
    assert depth == 1, "one layer"
    batch, seq, d_model = x_prompt.shape
    db, t_dec, _ = x_sample.shape
    n_pool = cache_k.shape[1]
    n_fox = cache_k.shape[3]
    n_heads = d_model // HEAD_DIM
    n_ret = n_heads - n_fox
    w_fox = n_fox * HEAD_DIM
    w_ret = n_ret * HEAD_DIM
    n_pages = page_table.shape[1]
    past_len = n_pages * PAGE_SIZE
    np_tok = batch * seq
    ns_tok = db * t_dec
    n_tok = np_tok + ns_tok
    alpha = (2.0 * depth) ** 0.25
    assert cache_k.shape[2] == PAGE_SIZE and cache_k.shape[4] == HEAD_DIM and t_dec % 8 == 0

    tm_big = _pick(n_tok, (768, 512, 384, 256, 128))
    tm_mid = _pick(n_tok, (256, 128))

    wi = w_in[0]
    f0 = 3 * w_fox
    w_main = jnp.concatenate([wi[:, :f0], wi[:, f0 + n_fox:]], axis=1).astype(BF16)
    wf_t = wi[:, f0:f0 + n_fox].T.astype(BF16)
    wo1 = w_o[0, :w_fox].astype(BF16)
    wo2 = w_o[0, w_fox:].astype(BF16)
    wq = w_pq[0].astype(BF16)
    keys = peer_keys[0].astype(BF16)
    u_b = peer_u[0].astype(BF16)
    v_b = peer_v[0].astype(BF16)
    wg = w_pg[0].astype(BF16)
    we = w_pe[0].astype(BF16)
    row = lambda a: a.reshape(1, -1)

    x_all = jnp.concatenate([x_prompt.reshape(np_tok, d_model), x_sample.reshape(ns_tok, d_model)], axis=0)
    xb = x_all.astype(BF16)
    p_all = jnp.concatenate([p_prompt[0].reshape(np_tok, -1), p_sample[0].reshape(ns_tok, -1)], axis=0)

    z = _matmul(xb, w_main, tm_big, _pick(w_main.shape[1], (1024, 512)), F32, "in_proj")
    logf_t = _log_forget_t(wf_t, xb, b_f[0].reshape(n_fox, 1), tm_big)

    c_t = _segment_cumsum(logf_t[:, :np_tok], seq)
    o_fox_p = _fox_prompt(z, c_t, batch, seq, n_fox)

    cn_t = _segment_cumsum(jnp.pad(logf_t[:, np_tok:], ((0, 0), (0, (-ns_tok) % 128))), t_dec)[:, :ns_tok]
    cn = cn_t.reshape(n_fox, db, t_dec).transpose(1, 0, 2)
    cn_pad = jnp.pad(cn, ((0, 0), (0, 0), (0, PAGE_SIZE - t_dec)), constant_values=BIG)
    suffix, totals = _page_suffix(cache_logf[0].reshape(n_pool, PAGE_SIZE * n_fox))
    o_fox_s = _fox_sample(z, np_tok, cn_pad, cache_k[0].reshape(n_pool, PAGE_SIZE, w_fox),
                          cache_v[0].reshape(n_pool, PAGE_SIZE, w_fox),
                          suffix.reshape(n_pool, n_fox, PAGE_SIZE), totals.reshape(n_pool, n_fox, PAGE_SIZE),
                          page_table, t_dec, n_fox)

    col_r = 3 * n_fox
    gn_row = row(gn_g[0])
    cos_p, sin_p = _rotary_tables(jnp.arange(seq, dtype=F32))
    cos_s, sin_s = _rotary_tables(past_len + jnp.arange(t_dec, dtype=F32))
    zeros_state = jnp.zeros((batch, n_ret, HEAD_DIM, HEAD_DIM), F32)
    o_ret_p, s_p = _retention(z, 0, col_r, n_ret, batch, seq, RET_CHUNK, zeros_state, cos_p, sin_p,
                              _retention_tables(n_ret, RET_CHUNK), gn_row)
    o_ret_s, s_s = _retention(z, np_tok, col_r, n_ret, db, t_dec, t_dec, state_ret[0], cos_s, sin_s,
                              _retention_tables(n_ret, t_dec), gn_row)

    o_fox = jnp.concatenate([o_fox_p, o_fox_s.astype(BF16)], axis=0)
    o_ret = jnp.concatenate([o_ret_p, o_ret_s.astype(BF16)], axis=0)
    h1, h1b = _wo_ln(x_all, o_fox, o_ret, wo1, wo2, row(ln1_g[0]), row(ln1_b[0]), alpha, tm_mid)
    a1, a2, th = _peer_scores(h1b, wq, keys, _pick(n_tok, (128,)))
    peer_out = _peer_dense(h1b, u_b, v_b, a1, a2, th, tm_big, 256)
    y = _out_stage(h1, peer_out, p_all, row(ln2_g[0]), row(ln2_b[0]), wg, row(b_pg[0]), we, alpha, tm_mid)

    k_p = z[:np_tok, w_fox:2 * w_fox].reshape(1, batch, seq, n_fox, HEAD_DIM)
    v_p = z[:np_tok, 2 * w_fox:3 * w_fox].reshape(1, batch, seq, n_fox, HEAD_DIM)
    k_s = z[np_tok:, w_fox:2 * w_fox].reshape(1, db, t_dec, n_fox, HEAD_DIM)
    v_s = z[np_tok:, 2 * w_fox:3 * w_fox].reshape(1, db, t_dec, n_fox, HEAD_DIM)
    logf = logf_t.T
    return (y[:np_tok].reshape(batch, seq, d_model), y[np_tok:].reshape(db, t_dec, d_model),
            k_p, v_p, logf[:np_tok].reshape(1, batch, seq, n_fox), s_p[None],
            k_s, v_s, logf[np_tok:].reshape(1, db, t_dec, n_fox), s_s[None])
```

```python
import functools
import math

import numpy as np
import jax
import jax.numpy as jnp
from jax import lax
from jax.experimental import pallas as pl
from jax.experimental.pallas import tpu as pltpu

F32 = jnp.float32
BF16 = jnp.bfloat16

HEAD_DIM = 128
PAGE_SIZE = 128
ROPE_BASE = 10000.0
RET_CHUNK = 128
PEER_HEADS = 8
PEER_KEYS = 128
PEER_HALF = 128
PEER_TOPK = 16
LN_EPS = 1e-5
GN_EPS = 1e-6
NEG_INF = float("-inf")
BIG = 1e30

VMEM_LIMIT = 56 * 1024 * 1024

_NT = (((1,), (1,)), ((), ()))
_TN = (((0,), (0,)), ((), ()))


def _pick(n, candidates):
    for c in candidates:
        if n % c == 0:
            return c
    raise ValueError(f"no tile in {candidates} divides {n}")


def _idiv(x, c):
    assert c & (c - 1) == 0
    return lax.shift_right_logical(x, int(math.log2(c)))


def _imod(x, c):
    assert c & (c - 1) == 0
    return x & (c - 1)


def _params(*sem):
    return pltpu.CompilerParams(dimension_semantics=sem, vmem_limit_bytes=VMEM_LIMIT)


def _mm_body(x_ref, w_ref, o_ref):
    o_ref[...] = jnp.dot(x_ref[...], w_ref[...], preferred_element_type=F32).astype(o_ref.dtype)


def _matmul(x, w, tm, tn, out_dtype, name):
    m, k = x.shape
    n = w.shape[1]
    return pl.pallas_call(
        _mm_body,
        grid=(m // tm, n // tn),
        in_specs=[pl.BlockSpec((tm, k), lambda i, j: (i, 0)), pl.BlockSpec((k, tn), lambda i, j: (0, j))],
        out_specs=pl.BlockSpec((tm, tn), lambda i, j: (i, j)),
        out_shape=jax.ShapeDtypeStruct((m, n), out_dtype),
        compiler_params=_params("parallel", "parallel"),
        name=name,
    )(x, w)


def _logf_body(wf_ref, x_ref, b_ref, o_ref):
    f = lax.dot_general(wf_ref[...], x_ref[...], _NT, preferred_element_type=F32) + b_ref[...]
    o_ref[...] = jnp.minimum(f, 0.0) - jnp.log1p(jnp.exp(-jnp.abs(f)))


def _log_forget_t(wf_t, xb, b_col, tm):
    h, k = wf_t.shape
    n = xb.shape[0]
    return pl.pallas_call(
        _logf_body,
        grid=(n // tm,),
        in_specs=[pl.BlockSpec((h, k), lambda i: (0, 0)), pl.BlockSpec((tm, k), lambda i: (i, 0)),
                  pl.BlockSpec((h, 1), lambda i: (0, 0))],
        out_specs=pl.BlockSpec((h, tm), lambda i: (0, i)),
        out_shape=jax.ShapeDtypeStruct((h, n), F32),
        compiler_params=_params("parallel"),
        name="log_forget",
    )(wf_t, xb, b_col)


def _split3(x):
    hi = x.astype(BF16)
    r1 = x - hi.astype(F32)
    mid = r1.astype(BF16)
    lo = (r1 - mid.astype(F32)).astype(BF16)
    return hi, mid, lo


def _dot3(parts, m):
    acc = jnp.dot(parts[0], m, preferred_element_type=F32)
    acc = acc + jnp.dot(parts[1], m, preferred_element_type=F32)
    return acc + jnp.dot(parts[2], m, preferred_element_type=F32)


def _cumsum_body(x_ref, o_ref, *, seg, blk):
    rows, length = x_ref.shape
    r = lax.broadcasted_iota(jnp.int32, (blk, blk), 0)
    c = lax.broadcasted_iota(jnp.int32, (blk, blk), 1)
    if seg >= blk:
        tri = (r <= c)
    else:
        tri = (r <= c) & (_idiv(r, seg) == _idiv(c, seg))
    tri = jnp.where(tri, 1.0, 0.0).astype(BF16)
    carry = jnp.zeros((rows, 1), F32)
    for j in range(length // blk):
        if seg < blk or (j * blk) % seg == 0:
            carry = jnp.zeros((rows, 1), F32)
        cs = _dot3(_split3(x_ref[:, j * blk:(j + 1) * blk]), tri) + carry
        o_ref[:, j * blk:(j + 1) * blk] = cs
        carry = cs[:, blk - 1:blk]


def _segment_cumsum(x, seg):
    rows, length = x.shape
    blk = _pick(length, (256, 128))
    assert seg % blk == 0 or blk % seg == 0
    return pl.pallas_call(
        functools.partial(_cumsum_body, seg=seg, blk=blk),
        out_shape=jax.ShapeDtypeStruct((rows, length), F32),
        compiler_params=pltpu.CompilerParams(vmem_limit_bytes=VMEM_LIMIT),
        name="segment_cumsum",
    )(x)


def _fox_prompt_body(q_ref, k_ref, v_ref, c_ref, o_ref, *, tq, scale):
    s_len = q_ref.shape[0]
    n_h = q_ref.shape[1] // HEAD_DIM
    row = lax.broadcasted_iota(jnp.int32, (tq, tq), 0)
    col = lax.broadcasted_iota(jnp.int32, (tq, tq), 1)
    causal = col <= row

    def step(qs, kj, carry, mask):
        ks = kj * tq if isinstance(kj, int) else pl.multiple_of(kj * tq, tq)
        out = []
        for h in range(n_h):
            m, l, acc = carry[h]
            lanes = slice(h * HEAD_DIM, (h + 1) * HEAD_DIM)
            k = k_ref[pl.ds(ks, tq), lanes].astype(BF16)
            v = v_ref[pl.ds(ks, tq), lanes].astype(BF16)
            s = lax.dot_general(qs[h], k, _NT, preferred_element_type=F32) - c_ref[h, kj]
            if mask:
                s = jnp.where(causal, s, NEG_INF)
            m_new = jnp.maximum(m, jnp.max(s, axis=1, keepdims=True))
            p = jnp.exp(s - m_new)
            alpha = jnp.exp(m - m_new)
            l = alpha * l + jnp.sum(p, axis=1, keepdims=True)
            acc = alpha * acc + jnp.dot(p.astype(BF16), v, preferred_element_type=F32)
            out.append((m_new, l, acc))
        return tuple(out)

    for qi in range(s_len // tq):
        qs = [(q_ref[qi * tq:(qi + 1) * tq, h * HEAD_DIM:(h + 1) * HEAD_DIM] * scale).astype(BF16)
              for h in range(n_h)]
        carry = tuple((jnp.full((tq, 1), NEG_INF, F32), jnp.zeros((tq, 1), F32), jnp.zeros((tq, HEAD_DIM), F32))
                      for _ in range(n_h))
        if qi > 0:
            carry = lax.fori_loop(0, qi, lambda kj, cr: step(qs, kj, cr, False), carry)
        carry = step(qs, qi, carry, True)
        for h in range(n_h):
            m, l, acc = carry[h]
            o_ref[qi * tq:(qi + 1) * tq, h * HEAD_DIM:(h + 1) * HEAD_DIM] = (acc / l).astype(o_ref.dtype)


FOX_HEADS_PER_STEP = 2


def _fox_prompt(z, c_t, batch, seq, n_fox):
    tq = _pick(seq, (256, 128))
    hp = FOX_HEADS_PER_STEP
    assert n_fox % hp == 0
    c4 = c_t.reshape(n_fox, batch * seq // tq, 1, tq)
    nb = seq // tq
    groups = n_fox // hp
    width = hp * HEAD_DIM
    return pl.pallas_call(
        functools.partial(_fox_prompt_body, tq=tq, scale=HEAD_DIM ** -0.5),
        grid=(batch, groups),
        in_specs=[
            pl.BlockSpec((seq, width), lambda b, h: (b, h)),
            pl.BlockSpec((seq, width), lambda b, h: (b, groups + h)),
            pl.BlockSpec((seq, width), lambda b, h: (b, 2 * groups + h)),
            pl.BlockSpec((hp, nb, 1, tq), lambda b, h: (h, b, 0, 0)),
        ],
        out_specs=pl.BlockSpec((seq, width), lambda b, h: (b, h)),
        out_shape=jax.ShapeDtypeStruct((batch * seq, n_fox * HEAD_DIM), BF16),
        compiler_params=_params("parallel", "parallel"),
        name="fox_prompt",
    )(z, z, z, c4)


def _page_suffix_body(lp_ref, et_ref):
    n = lp_ref.shape[1]
    n_h = n // PAGE_SIZE
    r = lax.broadcasted_iota(jnp.int32, (n, n), 0)
    c = lax.broadcasted_iota(jnp.int32, (n, n), 1)
    same_head = _imod(r, n_h) == _imod(c, n_h)
    later = _idiv(r, n_h) > _idiv(c, n_h)
    m_e = jnp.where(same_head & later, 1.0, 0.0).astype(BF16)
    m_t = jnp.where(same_head, 1.0, 0.0).astype(BF16)
    parts = _split3(lp_ref[...])
    et_ref[:, 0, :] = _dot3(parts, m_e)
    et_ref[:, 1, :] = _dot3(parts, m_t)


def _page_suffix(lp_rows):
    n_pool, n = lp_rows.shape
    tr = _pick(n_pool, (512, 256, 128, 64, 32, 16, 8))
    return pl.pallas_call(
        _page_suffix_body,
        grid=(n_pool // tr,),
        in_specs=[pl.BlockSpec((tr, n), lambda i: (i, 0))],
        out_specs=pl.BlockSpec((tr, 2, n), lambda i: (i, 0, 0)),
        out_shape=jax.ShapeDtypeStruct((n_pool, 2, n), F32),
        compiler_params=_params("parallel"),
        name="page_suffix",
    )(lp_rows)


def _rows_per_head(x, t):
    h, l = x.shape
    return jnp.broadcast_to(x[:, None, :], (h, t, l)).reshape(h * t, l)


def _fox_sample_body(pt_ref, q_ref, kn_ref, vn_ref, cn_ref, *refs, scale, n_fox, n_grp):
    k_refs = refs[:n_grp]
    v_refs = refs[n_grp:2 * n_grp]
    et_refs = refs[2 * n_grp:3 * n_grp]
    o_ref, q_sc, hm_sc, m_sc, l_sc, acc, carry = refs[3 * n_grp:]
    p = pl.program_id(1)
    t = q_ref.shape[0]
    rows = n_fox * t
    keys = PAGE_SIZE * n_fox

    @pl.when(p == 0)
    def _init():
        q_sc[...] = (jnp.concatenate([q_ref[:, h * HEAD_DIM:(h + 1) * HEAD_DIM] for h in range(n_fox)], axis=0)
                     * scale).astype(BF16)
        r = lax.broadcasted_iota(jnp.int32, (rows, keys), 0)
        c = lax.broadcasted_iota(jnp.int32, (rows, keys), 1)
        hm_sc[...] = jnp.where(_idiv(r, t) == _imod(c, n_fox), 0.0, NEG_INF)
        m_sc[...] = jnp.full(m_sc.shape, NEG_INF, F32)
        l_sc[...] = jnp.zeros(l_sc.shape, F32)
        acc[...] = jnp.zeros(acc.shape, F32)
        carry[...] = jnp.zeros(carry.shape, F32)

    q = q_sc[...]
    hm = hm_sc[...]
    c = carry[...]
    scores = []
    for g in range(n_grp):
        et = et_refs[g][...]
        kf = k_refs[g][...].reshape(keys, HEAD_DIM).astype(BF16)
        scores.append(lax.dot_general(q, kf, _NT, preferred_element_type=F32) + (et[0:1] + c) + hm)
        c = c + et[1:2]
    carry[...] = c
    m_old = m_sc[...]
    m_new = m_old
    for s in scores:
        m_new = jnp.maximum(m_new, jnp.max(s, axis=1, keepdims=True))
    alpha = jnp.exp(m_old - m_new)
    l_new = alpha * l_sc[...]
    acc_new = alpha * acc[...]
    for g in range(n_grp):
        pr = jnp.exp(scores[g] - m_new)
        l_new = l_new + jnp.sum(pr, axis=1, keepdims=True)
        vf = v_refs[g][...].reshape(keys, HEAD_DIM).astype(BF16)
        acc_new = acc_new + jnp.dot(pr.astype(BF16), vf, preferred_element_type=F32)
    m_sc[...] = m_new
    l_sc[...] = l_new
    acc[...] = acc_new

    @pl.when(p == pl.num_programs(1) - 1)
    def _fin():
        width = n_fox * HEAD_DIM
        r = lax.broadcasted_iota(jnp.int32, (rows, width), 0)
        cw = lax.broadcasted_iota(jnp.int32, (rows, width), 1)
        q_rep = jnp.concatenate([q_ref[...]] * n_fox, axis=0) * scale
        qbd = jnp.where(_idiv(r, t) == _idiv(cw, HEAD_DIM), q_rep, 0.0).astype(BF16)
        pad = jnp.zeros((PAGE_SIZE - t, width), BF16)
        k_new = jnp.concatenate([kn_ref[...].astype(BF16), pad], axis=0)
        v_new = jnp.concatenate([vn_ref[...].astype(BF16), pad], axis=0)
        s = lax.dot_general(qbd, k_new, _NT, preferred_element_type=F32) - _rows_per_head(cn_ref[...], t)
        rr = lax.broadcasted_iota(jnp.int32, (rows, PAGE_SIZE), 0)
        cc = lax.broadcasted_iota(jnp.int32, (rows, PAGE_SIZE), 1)
        s = jnp.where(cc <= _imod(rr, t), s, NEG_INF)
        m_fin = jnp.maximum(m_new, jnp.max(s, axis=1, keepdims=True))
        pr = jnp.exp(s - m_fin)
        a_fin = jnp.exp(m_new - m_fin)
        l_fin = a_fin * l_new + jnp.sum(pr, axis=1, keepdims=True)
        o_wide = jnp.dot(pr.astype(BF16), v_new, preferred_element_type=F32)
        for h in range(n_fox):
            blk = slice(h * t, (h + 1) * t)
            lanes = slice(h * HEAD_DIM, (h + 1) * HEAD_DIM)
            o_h = a_fin[blk] * acc_new[blk] + o_wide[blk, lanes]
            o_ref[:, lanes] = (o_h / l_fin[blk]).astype(o_ref.dtype)


def _fox_sample(z, row0, cn_pad, cache_k, cache_v, page_et, page_table, t, n_fox):
    db, n_pages = page_table.shape
    width = n_fox * HEAD_DIM
    keys = PAGE_SIZE * n_fox
    n_grp = _pick(n_pages, (8, 4, 2, 1))
    rb0 = row0 // t
    last = n_pages - 1

    def kv_spec(g):
        return pl.BlockSpec((None, None, PAGE_SIZE, n_fox, HEAD_DIM),
                            lambda b, p, pt: (0, pt[b, last - p * n_grp - g], 0, 0, 0))

    def et_spec(g):
        return pl.BlockSpec((None, 2, keys), lambda b, p, pt: (pt[b, last - p * n_grp - g], 0, 0))

    rng = range(n_grp)
    grid_spec = pltpu.PrefetchScalarGridSpec(
        num_scalar_prefetch=1,
        grid=(db, n_pages // n_grp),
        in_specs=[
            pl.BlockSpec((t, width), lambda b, p, pt: (rb0 + b, 0)),
            pl.BlockSpec((t, width), lambda b, p, pt: (rb0 + b, 1)),
            pl.BlockSpec((t, width), lambda b, p, pt: (rb0 + b, 2)),
            pl.BlockSpec((None, n_fox, PAGE_SIZE), lambda b, p, pt: (b, 0, 0)),
        ] + [kv_spec(g) for g in rng] + [kv_spec(g) for g in rng] + [et_spec(g) for g in rng],
        out_specs=pl.BlockSpec((t, width), lambda b, p, pt: (b, 0)),
        scratch_shapes=[
            pltpu.VMEM((n_fox * t, HEAD_DIM), BF16),
            pltpu.VMEM((n_fox * t, keys), F32),
            pltpu.VMEM((n_fox * t, 1), F32),
            pltpu.VMEM((n_fox * t, 1), F32),
            pltpu.VMEM((n_fox * t, HEAD_DIM), F32),
            pltpu.VMEM((1, keys), F32),
        ],
    )
    return pl.pallas_call(
        functools.partial(_fox_sample_body, scale=HEAD_DIM ** -0.5, n_fox=n_fox, n_grp=n_grp),
        grid_spec=grid_spec,
        out_shape=jax.ShapeDtypeStruct((db * t, width), F32),
        compiler_params=_params("parallel", "arbitrary"),
        name="fox_sample",
    )(page_table, z, z, z, cn_pad, *([cache_k] * n_grp), *([cache_v] * n_grp), *([page_et] * n_grp))


def _rotate(x, cos_f, sin_f):
    return x * cos_f + pltpu.roll(x, HEAD_DIM // 2, 1) * sin_f


def _retention_body(q_ref, k_ref, v_ref, g_ref, cos_ref, sin_ref, dec_ref, qd_ref, kd_ref, cd_ref, gn_ref,
                    s0_ref, o_ref, sout_ref, state, *, k_scale):
    c = pl.program_id(1)
    n_ret = state.shape[0]

    @pl.when(c == 0)
    def _():
        state[...] = s0_ref[...]

    cos_f = cos_ref[...]
    sin_f = sin_ref[...]
    for h in range(n_ret):
        lanes = slice(h * HEAD_DIM, (h + 1) * HEAD_DIM)
        q = _rotate(q_ref[:, lanes], cos_f, sin_f)
        k = _rotate(k_ref[:, lanes], cos_f, sin_f) * k_scale
        qb = q.astype(BF16)
        vb = v_ref[:, lanes].astype(BF16)
        s_in = state[h]
        scores = lax.dot_general(qb, k.astype(BF16), _NT, preferred_element_type=F32) * dec_ref[h]
        y = jnp.dot(scores.astype(BF16), vb, preferred_element_type=F32)
        y = y + jnp.dot(qb, s_in.astype(BF16), preferred_element_type=F32) * qd_ref[h]
        kd = (k * kd_ref[h]).astype(BF16)
        new_state = cd_ref[h] * s_in + lax.dot_general(kd, vb, _TN, preferred_element_type=F32)
        state[h] = new_state
        mu = jnp.mean(y, axis=1, keepdims=True)
        yc = y - mu
        var = jnp.mean(yc * yc, axis=1, keepdims=True)
        yn = yc * lax.rsqrt(var + GN_EPS) * gn_ref[:, lanes]
        g = g_ref[:, lanes]
        o_ref[:, lanes] = (g / (1.0 + jnp.exp(-g)) * yn).astype(o_ref.dtype)

    @pl.when(c == pl.num_programs(1) - 1)
    def _():
        sout_ref[...] = state[...]


def _retention(z, row0, col0, n_ret, batch, seq, chunk, state0, cos_f, sin_f, tables, gn_row):
    dec, qd, kd, cd = tables
    nc = seq // chunk
    rb0 = row0 // chunk
    d = HEAD_DIM
    width = n_ret * d
    assert col0 % width == 0
    cb0 = col0 // width
    zspec = lambda off: pl.BlockSpec((chunk, width), lambda b, c: (rb0 + b * nc + c, cb0 + off))
    full = lambda a: pl.BlockSpec(a.shape, lambda b, c: (0,) * a.ndim)
    out, s_fin = pl.pallas_call(
        functools.partial(_retention_body, k_scale=HEAD_DIM ** -0.5),
        grid=(batch, nc),
        in_specs=[
            zspec(0), zspec(1), zspec(2), zspec(3),
            pl.BlockSpec((chunk, d), lambda b, c: (c, 0)),
            pl.BlockSpec((chunk, d), lambda b, c: (c, 0)),
            full(dec), full(qd), full(kd), full(cd), full(gn_row),
            pl.BlockSpec((None, n_ret, d, d), lambda b, c: (b, 0, 0, 0)),
        ],
        out_specs=[
            pl.BlockSpec((chunk, width), lambda b, c: (b * nc + c, 0)),
            pl.BlockSpec((None, n_ret, d, d), lambda b, c: (b, 0, 0, 0)),
        ],
        out_shape=[
            jax.ShapeDtypeStruct((batch * seq, width), BF16 if chunk % 16 == 0 else F32),
            jax.ShapeDtypeStruct((batch, n_ret, d, d), F32),
        ],
        scratch_shapes=[pltpu.VMEM((n_ret, d, d), F32)],
        compiler_params=_params("parallel", "arbitrary"),
        name=f"retention_{chunk}",
    )(z, z, z, z, cos_f, sin_f, dec, qd, kd, cd, gn_row, state0)
    return out, s_fin


def _retention_tables(n_ret, chunk):
    log_gamma = jnp.log1p(-jnp.exp2(-5.0 - jnp.arange(n_ret, dtype=F32)))
    pos = jnp.arange(chunk, dtype=F32)
    diff = pos[:, None] - pos[None, :]
    dec = jnp.where((diff >= 0)[None], jnp.exp(jnp.maximum(diff, 0.0)[None] * log_gamma[:, None, None]), 0.0)
    qd = jnp.exp((pos[None, :] + 1.0) * log_gamma[:, None])
    kd = jnp.exp((chunk - 1.0 - pos)[None, :] * log_gamma[:, None])
    cd = jnp.exp(chunk * log_gamma)
    bc = lambda a: jnp.broadcast_to(a[:, :, None], (n_ret, chunk, HEAD_DIM))
    return dec, bc(qd), bc(kd), jnp.broadcast_to(cd[:, None, None], (n_ret, 1, HEAD_DIM))


def _rotary_tables(pos):
    half = HEAD_DIM // 2
    inv_freq = ROPE_BASE ** (-jnp.arange(half, dtype=F32) / half)
    ang = pos[:, None] * inv_freq[None, :]
    cos, sin = jnp.cos(ang), jnp.sin(ang)
    return jnp.concatenate([cos, cos], -1), jnp.concatenate([-sin, sin], -1)


def _layer_norm(x, g, b):
    mu = jnp.mean(x, axis=1, keepdims=True)
    xc = x - mu
    var = jnp.mean(xc * xc, axis=1, keepdims=True)
    return xc * lax.rsqrt(var + LN_EPS) * g + b


def _wo_ln_body(x_ref, of_ref, or_ref, w1_ref, w2_ref, g_ref, b_ref, h_ref, hb_ref, *, alpha):
    mixed = jnp.dot(of_ref[...], w1_ref[...], preferred_element_type=F32)
    mixed = mixed + jnp.dot(or_ref[...], w2_ref[...], preferred_element_type=F32)
    h = _layer_norm(alpha * x_ref[...] + mixed, g_ref[...], b_ref[...])
    h_ref[...] = h
    hb_ref[...] = h.astype(BF16)


def _wo_ln(x, o_fox, o_ret, w1, w2, g, b, alpha, tm):
    n, d = x.shape
    row = lambda w: pl.BlockSpec((tm, w), lambda i: (i, 0))
    full = lambda a: pl.BlockSpec(a.shape, lambda i: (0, 0))
    return pl.pallas_call(
        functools.partial(_wo_ln_body, alpha=alpha),
        grid=(n // tm,),
        in_specs=[row(d), row(o_fox.shape[1]), row(o_ret.shape[1]), full(w1), full(w2), full(g), full(b)],
        out_specs=[row(d), row(d)],
        out_shape=[jax.ShapeDtypeStruct((n, d), F32), jax.ShapeDtypeStruct((n, d), BF16)],
        compiler_params=_params("parallel"),
        name="wo_ln1",
    )(x, o_fox, o_ret, w1, w2, g, b)


def _top_desc(s, count):
    n = s.shape[0]
    iota = lax.broadcasted_iota(jnp.int32, s.shape, 0)
    vals = []
    work = s
    for _ in range(count):
        mj = jnp.max(work, axis=0, keepdims=True)
        vals.append(mj)
        first = jnp.min(jnp.where(work == mj, iota, n), axis=0, keepdims=True)
        work = jnp.where(iota == first, NEG_INF, work)
    return vals


_CAND = [(a, b) for a in range(PEER_TOPK + 1) for b in range(PEER_TOPK + 1) if (a + 1) * (b + 1) <= PEER_TOPK + 1]


def _peer_score_body(hb_ref, wq_ref, keys_ref, t1_ref, e1_ref, e2_ref):
    q = jnp.dot(hb_ref[...], wq_ref[...], preferred_element_type=F32).astype(BF16)
    tm = q.shape[0]
    for h in range(PEER_HEADS):
        s = []
        for c in range(2):
            col = (2 * h + c) * PEER_HALF
            s.append(lax.dot_general(keys_ref[h, c], q[:, col:col + PEER_HALF], _NT, preferred_element_type=F32))
        v1 = _top_desc(s[0], PEER_TOPK + 1)
        v2 = _top_desc(s[1], PEER_TOPK + 1)
        rows = [v1[a] + v2[b] for a, b in _CAND]
        pad = (-len(rows)) % 8
        cand = jnp.concatenate(rows + [jnp.full((pad, tm), NEG_INF, F32)], axis=0)
        top = _top_desc(cand, PEER_TOPK + 1)
        z = jnp.ones((1, tm), F32)
        for j in range(1, PEER_TOPK):
            z = z + jnp.exp(top[j] - top[0])
        theta = 0.5 * (top[PEER_TOPK - 1] + top[PEER_TOPK])
        c1 = s[0] - v1[0]
        t1_ref[h] = jnp.exp((theta - top[0]) - c1)
        e1_ref[h] = jnp.exp(c1 - jnp.log(z))
        e2_ref[h] = jnp.exp(s[1] - v2[0])


def _peer_scores(hb, wq, keys, tm):
    n, d = hb.shape
    return pl.pallas_call(
        _peer_score_body,
        grid=(n // tm,),
        in_specs=[pl.BlockSpec((tm, d), lambda i: (i, 0)), pl.BlockSpec(wq.shape, lambda i: (0, 0)),
                  pl.BlockSpec(keys.shape, lambda i: (0, 0, 0, 0))],
        out_specs=[pl.BlockSpec((PEER_HEADS, PEER_KEYS, tm), lambda i: (0, 0, i))] * 3,
        out_shape=[jax.ShapeDtypeStruct((PEER_HEADS, PEER_KEYS, n), F32)] * 3,
        compiler_params=_params("parallel"),
        name="peer_scores",
    )(hb, wq, keys)


PEER_SUB = 256


PEER_EB = 2 * PEER_SUB


def _peer_dense_body(hb_ref, u_ref, v0_ref, vp_ref, t1_ref, e1_ref, e2_ref, o_ref, act_sc, w_sc, acc_sc):
    j = pl.program_id(1)
    n_steps = pl.num_programs(1) - 1
    tm = hb_ref.shape[0]
    n_i1 = PEER_SUB // PEER_KEYS

    @pl.when(j == 0)
    def _():
        o_ref[...] = jnp.zeros(o_ref.shape, F32)
        acc_sc[...] = jnp.zeros(acc_sc.shape, F32)
        w_sc[1] = jnp.zeros(w_sc.shape[1:], BF16)

    d_model = o_ref.shape[1]
    n_tiles = tm // 128
    col_chunks = [slice(q * 256, (q + 1) * 256) for q in range(d_model // 256)]

    def product(slot, v_ref, cols):
        dst = acc_sc if slot == 1 else o_ref
        dst[:, cols] += jnp.dot(w_sc[slot], v_ref[:, cols], preferred_element_type=F32)

    def build_w(r, prev_slot, prev_v):
        act = lax.dot_general(hb_ref[...], u_ref[r * PEER_SUB:(r + 1) * PEER_SUB, :], _NT,
                              preferred_element_type=F32)
        act_sc[r] = 0.5 * act * (1.0 + lax.erf(act * (2.0 ** -0.5)))
        i1_0 = (2 * j + r) * n_i1
        t1_rows = [[t1_ref[h, pl.ds(i1_0 + k, 1), :] for k in range(n_i1)] for h in range(PEER_HEADS)]
        e1_rows = [[e1_ref[h, pl.ds(i1_0 + k, 1), :] for k in range(n_i1)] for h in range(PEER_HEADS)]
        pending_chunks = list(col_chunks)
        for c in range(n_tiles):
            lanes = slice(c * 128, (c + 1) * 128)
            gates = [None] * n_i1
            for h in range(PEER_HEADS):
                e2 = e2_ref[h, :, lanes]
                for k in range(n_i1):
                    g = jnp.where(e2 >= t1_rows[h][k][:, lanes], e2, 0.0) * e1_rows[h][k][:, lanes]
                    gates[k] = g if gates[k] is None else gates[k] + g
            for k in range(n_i1):
                cols = slice(k * PEER_KEYS, (k + 1) * PEER_KEYS)
                w_sc[r, lanes, cols] = (act_sc[r, lanes, cols] * gates[k].T).astype(BF16)
            n_now = -(-len(pending_chunks) // (n_tiles - c))
            for _ in range(n_now):
                product(prev_slot, prev_v, pending_chunks.pop(0))

    @pl.when(j < n_steps)
    def _main():
        build_w(0, 1, vp_ref)
        build_w(1, 0, v0_ref)

    @pl.when(j == n_steps)
    def _drain():
        for cols in col_chunks:
            product(1, vp_ref, cols)
        o_ref[...] += acc_sc[...]


def _peer_dense(hb, u, v, t1, e1, e2, tm):
    n, d = hb.shape
    n_steps = u.shape[0] // PEER_EB
    last = n_steps - 1
    per_tile = lambda shape, imap: pl.BlockSpec(shape, imap, pipeline_mode=pl.Buffered(1))
    gate_spec = lambda: per_tile((PEER_HEADS, PEER_KEYS, tm), lambda i, j: (0, 0, i))
    return pl.pallas_call(
        _peer_dense_body,
        grid=(n // tm, n_steps + 1),
        in_specs=[
            per_tile((tm, d), lambda i, j: (i, 0)),
            pl.BlockSpec((PEER_EB, d), lambda i, j: (jnp.minimum(j, last), 0)),
            pl.BlockSpec((PEER_SUB, d), lambda i, j: (2 * jnp.minimum(j, last), 0)),
            pl.BlockSpec((PEER_SUB, d), lambda i, j: (jnp.maximum(2 * j - 1, 0), 0)),
            gate_spec(), gate_spec(), gate_spec(),
        ],
        out_specs=pl.BlockSpec((tm, d), lambda i, j: (i, 0)),
        out_shape=jax.ShapeDtypeStruct((n, d), F32),
        scratch_shapes=[pltpu.VMEM((2, tm, PEER_SUB), F32), pltpu.VMEM((2, tm, PEER_SUB), BF16),
                        pltpu.VMEM((tm, d), F32)],
        compiler_params=_params("parallel", "arbitrary"),
        name="peer_dense",
    )(hb, u, v, v, t1, e1, e2)


def _out_body(h_ref, po_ref, p_ref, g_ref, b_ref, wg_ref, bg_ref, we_ref, y_ref, *, alpha):
    h2 = _layer_norm(alpha * h_ref[...] + po_ref[...], g_ref[...], b_ref[...])
    lin = jnp.dot(h2.astype(BF16), wg_ref[...], preferred_element_type=F32) + bg_ref[...]
    emb = jnp.dot(p_ref[...].astype(BF16), we_ref[...], preferred_element_type=F32)
    y_ref[...] = h2 + emb / (1.0 + jnp.exp(-lin))


def _out_stage(h1, peer_out, p_emb, g, b, wg, bg, we, alpha, tm):
    n, d = h1.shape
    row = lambda w: pl.BlockSpec((tm, w), lambda i: (i, 0))
    full = lambda a: pl.BlockSpec(a.shape, lambda i: (0, 0))
    return pl.pallas_call(
        functools.partial(_out_body, alpha=alpha),
        grid=(n // tm,),
        in_specs=[row(d), row(d), row(p_emb.shape[1]), full(g), full(b), full(wg), full(bg), full(we)],
        out_specs=row(d),
        out_shape=jax.ShapeDtypeStruct((n, d), F32),
        compiler_params=_params("parallel"),
        name="ln2_gate_out",
    )(h1, peer_out, p_emb, g, b, wg, bg, we)


def kernel(x_prompt, x_sample, cache_k, cache_v, cache_logf, state_ret, page_table, p_prompt, p_sample, w_in, b_f,
           gn_g, w_o, ln1_g, ln1_b, w_pq, peer_keys, peer_u, peer_v, ln2_g, ln2_b, w_pg, b_pg, w_pe):
    depth = w_in.shape[0]
    assert depth == 1, "one layer"
    batch, seq, d_model = x_prompt.shape
    db, t_dec, _ = x_sample.shape
    n_pool = cache_k.shape[1]
    n_fox = cache_k.shape[3]
    n_heads = d_model // HEAD_DIM
    n_ret = n_heads - n_fox
    w_fox = n_fox * HEAD_DIM
    w_ret = n_ret * HEAD_DIM
    n_pages = page_table.shape[1]
    past_len = n_pages * PAGE_SIZE
    np_tok = batch * seq
    ns_tok = db * t_dec
    n_tok = np_tok + ns_tok
    alpha = (2.0 * depth) ** 0.25
    assert cache_k.shape[2] == PAGE_SIZE and cache_k.shape[4] == HEAD_DIM and t_dec % 8 == 0

    tm_big = _pick(n_tok, (768, 512, 384, 256, 128))
    tm_mid = _pick(n_tok, (256, 128))

    wi = w_in[0]
    f0 = 3 * w_fox
    w_main = jnp.concatenate([wi[:, :f0], wi[:, f0 + n_fox:]], axis=1).astype(BF16)
    wf_t = wi[:, f0:f0 + n_fox].T.astype(BF16)
    wo1 = w_o[0, :w_fox].astype(BF16)
    wo2 = w_o[0, w_fox:].astype(BF16)
    wq = w_pq[0].astype(BF16)
    keys = peer_keys[0].astype(BF16)
    u_b = peer_u[0].astype(BF16)
    v_b = peer_v[0].astype(BF16)
    wg = w_pg[0].astype(BF16)
    we = w_pe[0].astype(BF16)
    row = lambda a: a.reshape(1, -1)

    x_all = jnp.concatenate([x_prompt.reshape(np_tok, d_model), x_sample.reshape(ns_tok, d_model)], axis=0)
    xb = x_all.astype(BF16)
    p_all = jnp.concatenate([p_prompt[0].reshape(np_tok, -1), p_sample[0].reshape(ns_tok, -1)], axis=0)

    z = _matmul(xb, w_main, tm_big, _pick(w_main.shape[1], (1024, 512)), F32, "in_proj")
    logf_t = _log_forget_t(wf_t, xb, b_f[0].reshape(n_fox, 1), tm_big)

    c_t = _segment_cumsum(logf_t[:, :np_tok], seq)
    o_fox_p = _fox_prompt(z, c_t, batch, seq, n_fox)

    cn_t = _segment_cumsum(jnp.pad(logf_t[:, np_tok:], ((0, 0), (0, (-ns_tok) % 128))), t_dec)[:, :ns_tok]
    cn = cn_t.reshape(n_fox, db, t_dec).transpose(1, 0, 2)
    cn_pad = jnp.pad(cn, ((0, 0), (0, 0), (0, PAGE_SIZE - t_dec)), constant_values=BIG)
    page_et = _page_suffix(cache_logf[0].reshape(n_pool, PAGE_SIZE * n_fox))
    o_fox_s = _fox_sample(z, np_tok, cn_pad, cache_k, cache_v, page_et, page_table, t_dec, n_fox)

    col_r = 3 * w_fox
    gn_row = row(gn_g[0])
    cos_p, sin_p = _rotary_tables(jnp.arange(seq, dtype=F32))
    cos_s, sin_s = _rotary_tables(past_len + jnp.arange(t_dec, dtype=F32))
    zeros_state = jnp.zeros((batch, n_ret, HEAD_DIM, HEAD_DIM), F32)
    o_ret_p, s_p = _retention(z, 0, col_r, n_ret, batch, seq, RET_CHUNK, zeros_state, cos_p, sin_p,
                              _retention_tables(n_ret, RET_CHUNK), gn_row)
    o_ret_s, s_s = _retention(z, np_tok, col_r, n_ret, db, t_dec, t_dec, state_ret[0], cos_s, sin_s,
                              _retention_tables(n_ret, t_dec), gn_row)

    o_fox = jnp.concatenate([o_fox_p, o_fox_s.astype(BF16)], axis=0)
    o_ret = jnp.concatenate([o_ret_p, o_ret_s.astype(BF16)], axis=0)
    h1, h1b = _wo_ln(x_all, o_fox, o_ret, wo1, wo2, row(ln1_g[0]), row(ln1_b[0]), alpha, tm_mid)
    t1, e1, e2 = _peer_scores(h1b, wq, keys, _pick(n_tok, (128,)))
    peer_out = _peer_dense(h1b, u_b, v_b, t1, e1, e2, tm_big)
    y = _out_stage(h1, peer_out, p_all, row(ln2_g[0]), row(ln2_b[0]), wg, row(b_pg[0]), we, alpha, tm_mid)

    k_p = z[:np_tok, w_fox:2 * w_fox].reshape(1, batch, seq, n_fox, HEAD_DIM)
    v_p = z[:np_tok, 2 * w_fox:3 * w_fox].reshape(1, batch, seq, n_fox, HEAD_DIM)
    k_s = z[np_tok:, w_fox:2 * w_fox].reshape(1, db, t_dec, n_fox, HEAD_DIM)
    v_s = z[np_tok:, 2 * w_fox:3 * w_fox].reshape(1, db, t_dec, n_fox, HEAD_DIM)
    logf = logf_t.T
    return (y[:np_tok].reshape(batch, seq, d_model), y[np_tok:].reshape(db, t_dec, d_model),
            k_p, v_p, logf[:np_tok].reshape(1, batch, seq, n_fox), s_p[None],
            k_s, v_s, logf[np_tok:].reshape(1, db, t_dec, n_fox), s_s[None])
```

```python
import functools
import math

import numpy as np
import jax
import jax.numpy as jnp
from jax import lax
from jax.experimental import pallas as pl
from jax.experimental.pallas import tpu as pltpu

F32 = jnp.float32
BF16 = jnp.bfloat16

HEAD_DIM = 128
PAGE_SIZE = 128
ROPE_BASE = 10000.0
RET_CHUNK = 128
PEER_HEADS = 8
PEER_KEYS = 128
PEER_HALF = 128
PEER_TOPK = 16
LN_EPS = 1e-5
GN_EPS = 1e-6
NEG_INF = float("-inf")
BIG = 1e30

VMEM_LIMIT = 56 * 1024 * 1024

_NT = (((1,), (1,)), ((), ()))
_TN = (((0,), (0,)), ((), ()))


def _pick(n, candidates):
    for c in candidates:
        if n % c == 0:
            return c
    raise ValueError(f"no tile in {candidates} divides {n}")


def _idiv(x, c):
    assert c & (c - 1) == 0
    return lax.shift_right_logical(x, int(math.log2(c)))


def _imod(x, c):
    assert c & (c - 1) == 0
    return x & (c - 1)


def _params(*sem):
    return pltpu.CompilerParams(dimension_semantics=sem, vmem_limit_bytes=VMEM_LIMIT)


def _mm_body(x_ref, w_ref, o_ref):
    o_ref[...] = jnp.dot(x_ref[...], w_ref[...], preferred_element_type=F32).astype(o_ref.dtype)


def _matmul(x, w, tm, tn, out_dtype, name):
    m, k = x.shape
    n = w.shape[1]
    return pl.pallas_call(
        _mm_body,
        grid=(m // tm, n // tn),
        in_specs=[pl.BlockSpec((tm, k), lambda i, j: (i, 0)), pl.BlockSpec((k, tn), lambda i, j: (0, j))],
        out_specs=pl.BlockSpec((tm, tn), lambda i, j: (i, j)),
        out_shape=jax.ShapeDtypeStruct((m, n), out_dtype),
        compiler_params=_params("parallel", "parallel"),
        name=name,
    )(x, w)


def _logf_body(wf_ref, x_ref, b_ref, o_ref):
    f = lax.dot_general(wf_ref[...], x_ref[...], _NT, preferred_element_type=F32) + b_ref[...]
    o_ref[...] = jnp.minimum(f, 0.0) - jnp.log1p(jnp.exp(-jnp.abs(f)))


def _log_forget_t(wf_t, xb, b_col, tm):
    h, k = wf_t.shape
    n = xb.shape[0]
    return pl.pallas_call(
        _logf_body,
        grid=(n // tm,),
        in_specs=[pl.BlockSpec((h, k), lambda i: (0, 0)), pl.BlockSpec((tm, k), lambda i: (i, 0)),
                  pl.BlockSpec((h, 1), lambda i: (0, 0))],
        out_specs=pl.BlockSpec((h, tm), lambda i: (0, i)),
        out_shape=jax.ShapeDtypeStruct((h, n), F32),
        compiler_params=_params("parallel"),
        name="log_forget",
    )(wf_t, xb, b_col)


def _split3(x):
    hi = x.astype(BF16)
    r1 = x - hi.astype(F32)
    mid = r1.astype(BF16)
    lo = (r1 - mid.astype(F32)).astype(BF16)
    return hi, mid, lo


def _dot3(parts, m):
    acc = jnp.dot(parts[0], m, preferred_element_type=F32)
    acc = acc + jnp.dot(parts[1], m, preferred_element_type=F32)
    return acc + jnp.dot(parts[2], m, preferred_element_type=F32)


def _cumsum_body(x_ref, o_ref, *, seg, blk):
    rows, length = x_ref.shape
    r = lax.broadcasted_iota(jnp.int32, (blk, blk), 0)
    c = lax.broadcasted_iota(jnp.int32, (blk, blk), 1)
    if seg >= blk:
        tri = (r <= c)
    else:
        tri = (r <= c) & (_idiv(r, seg) == _idiv(c, seg))
    tri = jnp.where(tri, 1.0, 0.0).astype(BF16)
    carry = jnp.zeros((rows, 1), F32)
    for j in range(length // blk):
        if seg < blk or (j * blk) % seg == 0:
            carry = jnp.zeros((rows, 1), F32)
        cs = _dot3(_split3(x_ref[:, j * blk:(j + 1) * blk]), tri) + carry
        o_ref[:, j * blk:(j + 1) * blk] = cs
        carry = cs[:, blk - 1:blk]


def _segment_cumsum(x, seg):
    rows, length = x.shape
    blk = _pick(length, (256, 128))
    assert seg % blk == 0 or blk % seg == 0
    return pl.pallas_call(
        functools.partial(_cumsum_body, seg=seg, blk=blk),
        out_shape=jax.ShapeDtypeStruct((rows, length), F32),
        compiler_params=pltpu.CompilerParams(vmem_limit_bytes=VMEM_LIMIT),
        name="segment_cumsum",
    )(x)


def _fox_prompt_body(q_ref, k_ref, v_ref, c_ref, o_ref, *, tq, scale):
    s_len = q_ref.shape[0]
    n_h = q_ref.shape[1] // HEAD_DIM
    row = lax.broadcasted_iota(jnp.int32, (tq, tq), 0)
    col = lax.broadcasted_iota(jnp.int32, (tq, tq), 1)
    causal = col <= row

    def step(qs, kj, carry, mask):
        ks = kj * tq if isinstance(kj, int) else pl.multiple_of(kj * tq, tq)
        out = []
        for h in range(n_h):
            m, l, acc = carry[h]
            lanes = slice(h * HEAD_DIM, (h + 1) * HEAD_DIM)
            k = k_ref[pl.ds(ks, tq), lanes].astype(BF16)
            v = v_ref[pl.ds(ks, tq), lanes].astype(BF16)
            s = lax.dot_general(qs[h], k, _NT, preferred_element_type=F32) - c_ref[h, kj]
            if mask:
                s = jnp.where(causal, s, NEG_INF)
            m_new = jnp.maximum(m, jnp.max(s, axis=1, keepdims=True))
            p = jnp.exp(s - m_new)
            alpha = jnp.exp(m - m_new)
            l = alpha * l + jnp.sum(p, axis=1, keepdims=True)
            acc = alpha * acc + jnp.dot(p.astype(BF16), v, preferred_element_type=F32)
            out.append((m_new, l, acc))
        return tuple(out)

    for qi in range(s_len // tq):
        qs = [(q_ref[qi * tq:(qi + 1) * tq, h * HEAD_DIM:(h + 1) * HEAD_DIM] * scale).astype(BF16)
              for h in range(n_h)]
        carry = tuple((jnp.full((tq, 1), NEG_INF, F32), jnp.zeros((tq, 1), F32), jnp.zeros((tq, HEAD_DIM), F32))
                      for _ in range(n_h))
        if qi > 0:
            carry = lax.fori_loop(0, qi, lambda kj, cr: step(qs, kj, cr, False), carry)
        carry = step(qs, qi, carry, True)
        for h in range(n_h):
            m, l, acc = carry[h]
            o_ref[qi * tq:(qi + 1) * tq, h * HEAD_DIM:(h + 1) * HEAD_DIM] = (acc / l).astype(o_ref.dtype)


FOX_HEADS_PER_STEP = 4


def _fox_prompt(z, c_t, batch, seq, n_fox):
    tq = _pick(seq, (256, 128))
    hp = FOX_HEADS_PER_STEP
    assert n_fox % hp == 0
    c4 = c_t.reshape(n_fox, batch * seq // tq, 1, tq)
    nb = seq // tq
    groups = n_fox // hp
    width = hp * HEAD_DIM
    return pl.pallas_call(
        functools.partial(_fox_prompt_body, tq=tq, scale=HEAD_DIM ** -0.5),
        grid=(batch, groups),
        in_specs=[
            pl.BlockSpec((seq, width), lambda b, h: (b, h)),
            pl.BlockSpec((seq, width), lambda b, h: (b, groups + h)),
            pl.BlockSpec((seq, width), lambda b, h: (b, 2 * groups + h)),
            pl.BlockSpec((hp, nb, 1, tq), lambda b, h: (h, b, 0, 0)),
        ],
        out_specs=pl.BlockSpec((seq, width), lambda b, h: (b, h)),
        out_shape=jax.ShapeDtypeStruct((batch * seq, n_fox * HEAD_DIM), BF16),
        compiler_params=_params("parallel", "parallel"),
        name="fox_prompt",
    )(z, z, z, c4)


def _page_suffix_body(lp_ref, et_ref):
    n = lp_ref.shape[1]
    n_h = n // PAGE_SIZE
    r = lax.broadcasted_iota(jnp.int32, (n, n), 0)
    c = lax.broadcasted_iota(jnp.int32, (n, n), 1)
    same_head = _imod(r, n_h) == _imod(c, n_h)
    later = _idiv(r, n_h) > _idiv(c, n_h)
    m_e = jnp.where(same_head & later, 1.0, 0.0).astype(BF16)
    m_t = jnp.where(same_head, 1.0, 0.0).astype(BF16)
    parts = _split3(lp_ref[...])
    et_ref[:, 0, :] = _dot3(parts, m_e)
    et_ref[:, 1, :] = _dot3(parts, m_t)


def _page_suffix(lp_rows):
    n_pool, n = lp_rows.shape
    tr = _pick(n_pool, (512, 256, 128, 64, 32, 16, 8))
    return pl.pallas_call(
        _page_suffix_body,
        grid=(n_pool // tr,),
        in_specs=[pl.BlockSpec((tr, n), lambda i: (i, 0))],
        out_specs=pl.BlockSpec((tr, 2, n), lambda i: (i, 0, 0)),
        out_shape=jax.ShapeDtypeStruct((n_pool, 2, n), F32),
        compiler_params=_params("parallel"),
        name="page_suffix",
    )(lp_rows)


def _rows_per_head(x, t):
    h, l = x.shape
    return jnp.broadcast_to(x[:, None, :], (h, t, l)).reshape(h * t, l)


def _fox_sample_body(pt_ref, q_ref, kn_ref, vn_ref, cn_ref, *refs, scale, n_fox, n_grp):
    k_refs = refs[:n_grp]
    v_refs = refs[n_grp:2 * n_grp]
    et_refs = refs[2 * n_grp:3 * n_grp]
    o_ref, q_sc, hm_sc, m_sc, l_sc, acc, carry = refs[3 * n_grp:]
    p = pl.program_id(1)
    t = q_ref.shape[0]
    rows = n_fox * t
    keys = PAGE_SIZE * n_fox

    @pl.when(p == 0)
    def _init():
        q_sc[...] = (jnp.concatenate([q_ref[:, h * HEAD_DIM:(h + 1) * HEAD_DIM] for h in range(n_fox)], axis=0)
                     * scale).astype(BF16)
        r = lax.broadcasted_iota(jnp.int32, (rows, keys), 0)
        c = lax.broadcasted_iota(jnp.int32, (rows, keys), 1)
        hm_sc[...] = jnp.where(_idiv(r, t) == _imod(c, n_fox), 0.0, NEG_INF)
        m_sc[...] = jnp.full(m_sc.shape, NEG_INF, F32)
        l_sc[...] = jnp.zeros(l_sc.shape, F32)
        acc[...] = jnp.zeros(acc.shape, F32)
        carry[...] = jnp.zeros(carry.shape, F32)

    q = q_sc[...]
    hm = hm_sc[...]
    c = carry[...]
    scores = []
    for g in range(n_grp):
        et = et_refs[g][...]
        kf = k_refs[g][...].reshape(keys, HEAD_DIM).astype(BF16)
        scores.append(lax.dot_general(q, kf, _NT, preferred_element_type=F32) + (et[0:1] + c) + hm)
        c = c + et[1:2]
    carry[...] = c
    m_old = m_sc[...]
    m_new = m_old
    for s in scores:
        m_new = jnp.maximum(m_new, jnp.max(s, axis=1, keepdims=True))
    alpha = jnp.exp(m_old - m_new)
    l_new = alpha * l_sc[...]
    acc_new = alpha * acc[...]
    for g in range(n_grp):
        pr = jnp.exp(scores[g] - m_new)
        l_new = l_new + jnp.sum(pr, axis=1, keepdims=True)
        vf = v_refs[g][...].reshape(keys, HEAD_DIM).astype(BF16)
        acc_new = acc_new + jnp.dot(pr.astype(BF16), vf, preferred_element_type=F32)
    m_sc[...] = m_new
    l_sc[...] = l_new
    acc[...] = acc_new

    @pl.when(p == pl.num_programs(1) - 1)
    def _fin():
        width = n_fox * HEAD_DIM
        r = lax.broadcasted_iota(jnp.int32, (rows, width), 0)
        cw = lax.broadcasted_iota(jnp.int32, (rows, width), 1)
        q_rep = jnp.concatenate([q_ref[...]] * n_fox, axis=0) * scale
        qbd = jnp.where(_idiv(r, t) == _idiv(cw, HEAD_DIM), q_rep, 0.0).astype(BF16)
        pad = jnp.zeros((PAGE_SIZE - t, width), BF16)
        k_new = jnp.concatenate([kn_ref[...].astype(BF16), pad], axis=0)
        v_new = jnp.concatenate([vn_ref[...].astype(BF16), pad], axis=0)
        s = lax.dot_general(qbd, k_new, _NT, preferred_element_type=F32) - _rows_per_head(cn_ref[...], t)
        rr = lax.broadcasted_iota(jnp.int32, (rows, PAGE_SIZE), 0)
        cc = lax.broadcasted_iota(jnp.int32, (rows, PAGE_SIZE), 1)
        s = jnp.where(cc <= _imod(rr, t), s, NEG_INF)
        m_fin = jnp.maximum(m_new, jnp.max(s, axis=1, keepdims=True))
        pr = jnp.exp(s - m_fin)
        a_fin = jnp.exp(m_new - m_fin)
        l_fin = a_fin * l_new + jnp.sum(pr, axis=1, keepdims=True)
        o_wide = jnp.dot(pr.astype(BF16), v_new, preferred_element_type=F32)
        for h in range(n_fox):
            blk = slice(h * t, (h + 1) * t)
            lanes = slice(h * HEAD_DIM, (h + 1) * HEAD_DIM)
            o_h = a_fin[blk] * acc_new[blk] + o_wide[blk, lanes]
            o_ref[:, lanes] = (o_h / l_fin[blk]).astype(o_ref.dtype)


def _fox_sample(z, row0, cn_pad, cache_k, cache_v, page_et, page_table, t, n_fox):
    db, n_pages = page_table.shape
    width = n_fox * HEAD_DIM
    keys = PAGE_SIZE * n_fox
    n_grp = _pick(n_pages, (16, 8, 4, 2, 1))
    rb0 = row0 // t
    last = n_pages - 1

    def kv_spec(g):
        return pl.BlockSpec((None, None, PAGE_SIZE, n_fox, HEAD_DIM),
                            lambda b, p, pt: (0, pt[b, last - p * n_grp - g], 0, 0, 0))

    def et_spec(g):
        return pl.BlockSpec((None, 2, keys), lambda b, p, pt: (pt[b, last - p * n_grp - g], 0, 0))

    rng = range(n_grp)
    grid_spec = pltpu.PrefetchScalarGridSpec(
        num_scalar_prefetch=1,
        grid=(db, n_pages // n_grp),
        in_specs=[
            pl.BlockSpec((t, width), lambda b, p, pt: (rb0 + b, 0)),
            pl.BlockSpec((t, width), lambda b, p, pt: (rb0 + b, 1)),
            pl.BlockSpec((t, width), lambda b, p, pt: (rb0 + b, 2)),
            pl.BlockSpec((None, n_fox, PAGE_SIZE), lambda b, p, pt: (b, 0, 0)),
        ] + [kv_spec(g) for g in rng] + [kv_spec(g) for g in rng] + [et_spec(g) for g in rng],
        out_specs=pl.BlockSpec((t, width), lambda b, p, pt: (b, 0)),
        scratch_shapes=[
            pltpu.VMEM((n_fox * t, HEAD_DIM), BF16),
            pltpu.VMEM((n_fox * t, keys), F32),
            pltpu.VMEM((n_fox * t, 1), F32),
            pltpu.VMEM((n_fox * t, 1), F32),
            pltpu.VMEM((n_fox * t, HEAD_DIM), F32),
            pltpu.VMEM((1, keys), F32),
        ],
    )
    return pl.pallas_call(
        functools.partial(_fox_sample_body, scale=HEAD_DIM ** -0.5, n_fox=n_fox, n_grp=n_grp),
        grid_spec=grid_spec,
        out_shape=jax.ShapeDtypeStruct((db * t, width), F32),
        compiler_params=_params("parallel", "arbitrary"),
        name="fox_sample",
    )(page_table, z, z, z, cn_pad, *([cache_k] * n_grp), *([cache_v] * n_grp), *([page_et] * n_grp))


def _rotate(x, cos_f, sin_f):
    return x * cos_f + pltpu.roll(x, HEAD_DIM // 2, 1) * sin_f


def _retention_body(q_ref, k_ref, v_ref, g_ref, cos_ref, sin_ref, dec_ref, qd_ref, kd_ref, cd_ref, gn_ref,
                    s0_ref, o_ref, sout_ref, state, *, k_scale):
    c = pl.program_id(1)
    n_ret = state.shape[0]

    @pl.when(c == 0)
    def _():
        state[...] = s0_ref[...]

    cos_f = cos_ref[...]
    sin_f = sin_ref[...]
    for h in range(n_ret):
        lanes = slice(h * HEAD_DIM, (h + 1) * HEAD_DIM)
        q = _rotate(q_ref[:, lanes], cos_f, sin_f)
        k = _rotate(k_ref[:, lanes], cos_f, sin_f) * k_scale
        qb = q.astype(BF16)
        vb = v_ref[:, lanes].astype(BF16)
        s_in = state[h]
        scores = lax.dot_general(qb, k.astype(BF16), _NT, preferred_element_type=F32) * dec_ref[h]
        y = jnp.dot(scores.astype(BF16), vb, preferred_element_type=F32)
        y = y + jnp.dot(qb, s_in.astype(BF16), preferred_element_type=F32) * qd_ref[h]
        kd = (k * kd_ref[h]).astype(BF16)
        new_state = cd_ref[h] * s_in + lax.dot_general(kd, vb, _TN, preferred_element_type=F32)
        state[h] = new_state
        mu = jnp.mean(y, axis=1, keepdims=True)
        yc = y - mu
        var = jnp.mean(yc * yc, axis=1, keepdims=True)
        yn = yc * lax.rsqrt(var + GN_EPS) * gn_ref[:, lanes]
        g = g_ref[:, lanes]
        o_ref[:, lanes] = (g / (1.0 + jnp.exp(-g)) * yn).astype(o_ref.dtype)

    @pl.when(c == pl.num_programs(1) - 1)
    def _():
        sout_ref[...] = state[...]


def _retention(z, row0, col0, n_ret, batch, seq, chunk, state0, cos_f, sin_f, tables, gn_row):
    dec, qd, kd, cd = tables
    nc = seq // chunk
    rb0 = row0 // chunk
    d = HEAD_DIM
    width = n_ret * d
    assert col0 % width == 0
    cb0 = col0 // width
    zspec = lambda off: pl.BlockSpec((chunk, width), lambda b, c: (rb0 + b * nc + c, cb0 + off))
    full = lambda a: pl.BlockSpec(a.shape, lambda b, c: (0,) * a.ndim)
    out, s_fin = pl.pallas_call(
        functools.partial(_retention_body, k_scale=HEAD_DIM ** -0.5),
        grid=(batch, nc),
        in_specs=[
            zspec(0), zspec(1), zspec(2), zspec(3),
            pl.BlockSpec((chunk, d), lambda b, c: (c, 0)),
            pl.BlockSpec((chunk, d), lambda b, c: (c, 0)),
            full(dec), full(qd), full(kd), full(cd), full(gn_row),
            pl.BlockSpec((None, n_ret, d, d), lambda b, c: (b, 0, 0, 0)),
        ],
        out_specs=[
            pl.BlockSpec((chunk, width), lambda b, c: (b * nc + c, 0)),
            pl.BlockSpec((None, n_ret, d, d), lambda b, c: (b, 0, 0, 0)),
        ],
        out_shape=[
            jax.ShapeDtypeStruct((batch * seq, width), BF16 if chunk % 16 == 0 else F32),
            jax.ShapeDtypeStruct((batch, n_ret, d, d), F32),
        ],
        scratch_shapes=[pltpu.VMEM((n_ret, d, d), F32)],
        compiler_params=_params("parallel", "arbitrary"),
        name=f"retention_{chunk}",
    )(z, z, z, z, cos_f, sin_f, dec, qd, kd, cd, gn_row, state0)
    return out, s_fin


def _retention_tables(n_ret, chunk):
    log_gamma = jnp.log1p(-jnp.exp2(-5.0 - jnp.arange(n_ret, dtype=F32)))
    pos = jnp.arange(chunk, dtype=F32)
    diff = pos[:, None] - pos[None, :]
    dec = jnp.where((diff >= 0)[None], jnp.exp(jnp.maximum(diff, 0.0)[None] * log_gamma[:, None, None]), 0.0)
    qd = jnp.exp((pos[None, :] + 1.0) * log_gamma[:, None])
    kd = jnp.exp((chunk - 1.0 - pos)[None, :] * log_gamma[:, None])
    cd = jnp.exp(chunk * log_gamma)
    bc = lambda a: jnp.broadcast_to(a[:, :, None], (n_ret, chunk, HEAD_DIM))
    return dec, bc(qd), bc(kd), jnp.broadcast_to(cd[:, None, None], (n_ret, 1, HEAD_DIM))


def _rotary_tables(pos):
    half = HEAD_DIM // 2
    inv_freq = ROPE_BASE ** (-jnp.arange(half, dtype=F32) / half)
    ang = pos[:, None] * inv_freq[None, :]
    cos, sin = jnp.cos(ang), jnp.sin(ang)
    return jnp.concatenate([cos, cos], -1), jnp.concatenate([-sin, sin], -1)


def _layer_norm(x, g, b):
    mu = jnp.mean(x, axis=1, keepdims=True)
    xc = x - mu
    var = jnp.mean(xc * xc, axis=1, keepdims=True)
    return xc * lax.rsqrt(var + LN_EPS) * g + b


def _wo_ln_body(x_ref, of_ref, or_ref, w1_ref, w2_ref, g_ref, b_ref, h_ref, hb_ref, *, alpha):
    mixed = jnp.dot(of_ref[...], w1_ref[...], preferred_element_type=F32)
    mixed = mixed + jnp.dot(or_ref[...], w2_ref[...], preferred_element_type=F32)
    h = _layer_norm(alpha * x_ref[...] + mixed, g_ref[...], b_ref[...])
    h_ref[...] = h
    hb_ref[...] = h.astype(BF16)


def _wo_ln(x, o_fox, o_ret, w1, w2, g, b, alpha, tm):
    n, d = x.shape
    row = lambda w: pl.BlockSpec((tm, w), lambda i: (i, 0))
    full = lambda a: pl.BlockSpec(a.shape, lambda i: (0, 0))
    return pl.pallas_call(
        functools.partial(_wo_ln_body, alpha=alpha),
        grid=(n // tm,),
        in_specs=[row(d), row(o_fox.shape[1]), row(o_ret.shape[1]), full(w1), full(w2), full(g), full(b)],
        out_specs=[row(d), row(d)],
        out_shape=[jax.ShapeDtypeStruct((n, d), F32), jax.ShapeDtypeStruct((n, d), BF16)],
        compiler_params=_params("parallel"),
        name="wo_ln1",
    )(x, o_fox, o_ret, w1, w2, g, b)


def _top_desc(s, count, n_masked, one_per_round):
    n = s.shape[0]

    vals, work = [], s
    if one_per_round:
        iota = lax.broadcasted_iota(jnp.int32, s.shape, 0)
        for _ in range(count):
            mj = jnp.max(work, axis=0, keepdims=True)
            vals.append(mj)
            first = jnp.min(jnp.where(work == mj, iota, n), axis=0, keepdims=True)
            work = jnp.where(iota == first, NEG_INF, work)
        return vals, jnp.zeros((1, s.shape[1]), F32)
    for _ in range(count):
        mj = jnp.max(work, axis=0, keepdims=True)
        vals.append(mj)
        work = jnp.where(work == mj, NEG_INF, work)
    removed = jnp.sum(jnp.where(work == NEG_INF, 1.0, 0.0), axis=0, keepdims=True)
    return vals, jnp.abs(removed - float(count + n_masked))


_CAND = [(a, b) for a in range(PEER_TOPK + 1) for b in range(PEER_TOPK + 1) if (a + 1) * (b + 1) <= PEER_TOPK + 1]


def _peer_score_body(hb_ref, wq_ref, keys_ref, t1_ref, e1_ref, e2_ref, q_sc):
    tm = hb_ref.shape[0]
    q_sc[...] = jnp.dot(hb_ref[...], wq_ref[...], preferred_element_type=F32).astype(BF16)

    def head(h, one_per_round):
        s = []
        for c in range(2):
            col = (2 * h + c) * PEER_HALF
            s.append(lax.dot_general(keys_ref[h, c], q_sc[:, col:col + PEER_HALF], _NT,
                                     preferred_element_type=F32))
        v1, bad1 = _top_desc(s[0], PEER_TOPK + 1, 0, one_per_round)
        v2, bad2 = _top_desc(s[1], PEER_TOPK + 1, 0, one_per_round)
        rows = [v1[a] + v2[b] for a, b in _CAND]
        pad = (-len(rows)) % 8
        cand = jnp.concatenate(rows + [jnp.full((pad, tm), NEG_INF, F32)], axis=0)
        top, bad3 = _top_desc(cand, PEER_TOPK + 1, pad, one_per_round)
        z = jnp.ones((1, tm), F32)
        for j in range(1, PEER_TOPK):
            z = z + jnp.exp(top[j] - top[0])
        theta = 0.5 * (top[PEER_TOPK - 1] + top[PEER_TOPK])
        c1 = s[0] - v1[0]
        t1_ref[h] = jnp.exp((theta - top[0]) - c1)
        e1_ref[h] = jnp.exp(c1 - jnp.log(z))
        e2_ref[h] = jnp.exp(s[1] - v2[0])
        return jnp.max(bad1 + bad2 + bad3)

    bad = [head(h, False) for h in range(PEER_HEADS)]
    for h in range(PEER_HEADS):
        @pl.when(bad[h] > 0.0)
        def _():
            head(h, True)


def _peer_scores(hb, wq, keys, tm):
    n, d = hb.shape
    return pl.pallas_call(
        _peer_score_body,
        grid=(n // tm,),
        in_specs=[pl.BlockSpec((tm, d), lambda i: (i, 0)), pl.BlockSpec(wq.shape, lambda i: (0, 0)),
                  pl.BlockSpec(keys.shape, lambda i: (0, 0, 0, 0))],
        out_specs=[pl.BlockSpec((PEER_HEADS, PEER_KEYS, tm), lambda i: (0, 0, i))] * 3,
        out_shape=[jax.ShapeDtypeStruct((PEER_HEADS, PEER_KEYS, n), F32)] * 3,
        scratch_shapes=[pltpu.VMEM((tm, wq.shape[1]), BF16)],
        compiler_params=_params("parallel"),
        name="peer_scores",
    )(hb, wq, keys)


PEER_SUB = 256


PEER_EB = 2 * PEER_SUB


def _peer_dense_body(hb_ref, u_ref, v0_ref, vp_ref, t1_ref, e1_ref, e2_ref, o_ref, act_sc, w_sc, acc_sc):
    j = pl.program_id(1)
    n_steps = pl.num_programs(1) - 1
    tm = hb_ref.shape[0]
    n_i1 = PEER_SUB // PEER_KEYS

    @pl.when(j == 0)
    def _():
        o_ref[...] = jnp.zeros(o_ref.shape, F32)
        acc_sc[...] = jnp.zeros(acc_sc.shape, F32)
        w_sc[1] = jnp.zeros(w_sc.shape[1:], BF16)

    d_model = o_ref.shape[1]
    n_tiles = tm // 128
    col_chunks = [slice(q * 256, (q + 1) * 256) for q in range(d_model // 256)]

    def product(slot, v_ref, cols):
        dst = acc_sc if slot == 1 else o_ref
        dst[:, cols] += jnp.dot(w_sc[slot], v_ref[:, cols], preferred_element_type=F32)

    def build_w(r, prev_slot, prev_v):
        act = lax.dot_general(hb_ref[...], u_ref[r * PEER_SUB:(r + 1) * PEER_SUB, :], _NT,
                              preferred_element_type=F32)
        act_sc[r] = 0.5 * act * (1.0 + lax.erf(act * (2.0 ** -0.5)))
        i1_0 = (2 * j + r) * n_i1
        t1_rows = [[t1_ref[h, pl.ds(i1_0 + k, 1), :] for k in range(n_i1)] for h in range(PEER_HEADS)]
        e1_rows = [[e1_ref[h, pl.ds(i1_0 + k, 1), :] for k in range(n_i1)] for h in range(PEER_HEADS)]
        pending_chunks = list(col_chunks)
        for c in range(n_tiles):
            lanes = slice(c * 128, (c + 1) * 128)
            gates = [None] * n_i1
            for h in range(PEER_HEADS):
                e2 = e2_ref[h, :, lanes]
                for k in range(n_i1):
                    g = jnp.where(e2 >= t1_rows[h][k][:, lanes], e2, 0.0) * e1_rows[h][k][:, lanes]
                    gates[k] = g if gates[k] is None else gates[k] + g
            for k in range(n_i1):
                cols = slice(k * PEER_KEYS, (k + 1) * PEER_KEYS)
                w_sc[r, lanes, cols] = (act_sc[r, lanes, cols] * gates[k].T).astype(BF16)
            n_now = -(-len(pending_chunks) // (n_tiles - c))
            for _ in range(n_now):
                product(prev_slot, prev_v, pending_chunks.pop(0))

    @pl.when(j < n_steps)
    def _main():
        build_w(0, 1, vp_ref)
        build_w(1, 0, v0_ref)

    @pl.when(j == n_steps)
    def _drain():
        for cols in col_chunks:
            product(1, vp_ref, cols)
        o_ref[...] += acc_sc[...]


def _peer_dense(hb, u, v, t1, e1, e2, tm):
    n, d = hb.shape
    n_steps = u.shape[0] // PEER_EB
    last = n_steps - 1
    per_tile = lambda shape, imap: pl.BlockSpec(shape, imap, pipeline_mode=pl.Buffered(1))
    gate_spec = lambda: per_tile((PEER_HEADS, PEER_KEYS, tm), lambda i, j: (0, 0, i))
    return pl.pallas_call(
        _peer_dense_body,
        grid=(n // tm, n_steps + 1),
        in_specs=[
            per_tile((tm, d), lambda i, j: (i, 0)),
            pl.BlockSpec((PEER_EB, d), lambda i, j: (jnp.minimum(j, last), 0)),
            pl.BlockSpec((PEER_SUB, d), lambda i, j: (2 * jnp.minimum(j, last), 0)),
            pl.BlockSpec((PEER_SUB, d), lambda i, j: (jnp.maximum(2 * j - 1, 0), 0)),
            gate_spec(), gate_spec(), gate_spec(),
        ],
        out_specs=pl.BlockSpec((tm, d), lambda i, j: (i, 0)),
        out_shape=jax.ShapeDtypeStruct((n, d), F32),
        scratch_shapes=[pltpu.VMEM((2, tm, PEER_SUB), F32), pltpu.VMEM((2, tm, PEER_SUB), BF16),
                        pltpu.VMEM((tm, d), F32)],
        compiler_params=_params("parallel", "arbitrary"),
        name="peer_dense",
    )(hb, u, v, v, t1, e1, e2)


def _out_body(h_ref, po_ref, p_ref, g_ref, b_ref, wg_ref, bg_ref, we_ref, y_ref, *, alpha):
    h2 = _layer_norm(alpha * h_ref[...] + po_ref[...], g_ref[...], b_ref[...])
    lin = jnp.dot(h2.astype(BF16), wg_ref[...], preferred_element_type=F32) + bg_ref[...]
    emb = jnp.dot(p_ref[...].astype(BF16), we_ref[...], preferred_element_type=F32)
    y_ref[...] = h2 + emb / (1.0 + jnp.exp(-lin))


def _out_stage(h1, peer_out, p_emb, g, b, wg, bg, we, alpha, tm):
    n, d = h1.shape
    row = lambda w: pl.BlockSpec((tm, w), lambda i: (i, 0))
    full = lambda a: pl.BlockSpec(a.shape, lambda i: (0, 0))
    return pl.pallas_call(
        functools.partial(_out_body, alpha=alpha),
        grid=(n // tm,),
        in_specs=[row(d), row(d), row(p_emb.shape[1]), full(g), full(b), full(wg), full(bg), full(we)],
        out_specs=row(d),
        out_shape=jax.ShapeDtypeStruct((n, d), F32),
        compiler_params=_params("parallel"),
        name="ln2_gate_out",
    )(h1, peer_out, p_emb, g, b, wg, bg, we)


def kernel(x_prompt, x_sample, cache_k, cache_v, cache_logf, state_ret, page_table, p_prompt, p_sample, w_in, b_f,
           gn_g, w_o, ln1_g, ln1_b, w_pq, peer_keys, peer_u, peer_v, ln2_g, ln2_b, w_pg, b_pg, w_pe):
    depth = w_in.shape[0]
    assert depth == 1, "one layer"
    batch, seq, d_model = x_prompt.shape
    db, t_dec, _ = x_sample.shape
    n_pool = cache_k.shape[1]
    n_fox = cache_k.shape[3]
    n_heads = d_model // HEAD_DIM
    n_ret = n_heads - n_fox
    w_fox = n_fox * HEAD_DIM
    w_ret = n_ret * HEAD_DIM
    n_pages = page_table.shape[1]
    past_len = n_pages * PAGE_SIZE
    np_tok = batch * seq
    ns_tok = db * t_dec
    n_tok = np_tok + ns_tok
    alpha = (2.0 * depth) ** 0.25
    assert cache_k.shape[2] == PAGE_SIZE and cache_k.shape[4] == HEAD_DIM and t_dec % 8 == 0

    tm_big = _pick(n_tok, (768, 512, 384, 256, 128))
    tm_mid = _pick(n_tok, (256, 128))

    wi = w_in[0]
    f0 = 3 * w_fox
    w_main = jnp.concatenate([wi[:, :f0], wi[:, f0 + n_fox:]], axis=1).astype(BF16)
    wf_t = wi[:, f0:f0 + n_fox].T.astype(BF16)
    wo1 = w_o[0, :w_fox].astype(BF16)
    wo2 = w_o[0, w_fox:].astype(BF16)
    wq = w_pq[0].astype(BF16)
    keys = peer_keys[0].astype(BF16)
    u_b = peer_u[0].astype(BF16)
    v_b = peer_v[0].astype(BF16)
    wg = w_pg[0].astype(BF16)
    we = w_pe[0].astype(BF16)
    row = lambda a: a.reshape(1, -1)

    x_all = jnp.concatenate([x_prompt.reshape(np_tok, d_model), x_sample.reshape(ns_tok, d_model)], axis=0)
    xb = x_all.astype(BF16)
    p_all = jnp.concatenate([p_prompt[0].reshape(np_tok, -1), p_sample[0].reshape(ns_tok, -1)], axis=0)

    z = _matmul(xb, w_main, tm_big, _pick(w_main.shape[1], (1024, 512)), F32, "in_proj")
    logf_t = _log_forget_t(wf_t, xb, b_f[0].reshape(n_fox, 1), tm_big)

    c_t = _segment_cumsum(logf_t[:, :np_tok], seq)
    o_fox_p = _fox_prompt(z, c_t, batch, seq, n_fox)

    cn_t = _segment_cumsum(jnp.pad(logf_t[:, np_tok:], ((0, 0), (0, (-ns_tok) % 128))), t_dec)[:, :ns_tok]
    cn = cn_t.reshape(n_fox, db, t_dec).transpose(1, 0, 2)
    cn_pad = jnp.pad(cn, ((0, 0), (0, 0), (0, PAGE_SIZE - t_dec)), constant_values=BIG)
    page_et = _page_suffix(cache_logf[0].reshape(n_pool, PAGE_SIZE * n_fox))
    o_fox_s = _fox_sample(z, np_tok, cn_pad, cache_k, cache_v, page_et, page_table, t_dec, n_fox)

    col_r = 3 * w_fox
    gn_row = row(gn_g[0])
    cos_p, sin_p = _rotary_tables(jnp.arange(seq, dtype=F32))
    cos_s, sin_s = _rotary_tables(past_len + jnp.arange(t_dec, dtype=F32))
    zeros_state = jnp.zeros((batch, n_ret, HEAD_DIM, HEAD_DIM), F32)
    o_ret_p, s_p = _retention(z, 0, col_r, n_ret, batch, seq, RET_CHUNK, zeros_state, cos_p, sin_p,
                              _retention_tables(n_ret, RET_CHUNK), gn_row)
    o_ret_s, s_s = _retention(z, np_tok, col_r, n_ret, db, t_dec, t_dec, state_ret[0], cos_s, sin_s,
                              _retention_tables(n_ret, t_dec), gn_row)

    o_fox = jnp.concatenate([o_fox_p, o_fox_s.astype(BF16)], axis=0)
    o_ret = jnp.concatenate([o_ret_p, o_ret_s.astype(BF16)], axis=0)
    h1, h1b = _wo_ln(x_all, o_fox, o_ret, wo1, wo2, row(ln1_g[0]), row(ln1_b[0]), alpha, tm_mid)
    t1, e1, e2 = _peer_scores(h1b, wq, keys, _pick(n_tok, (128,)))
    peer_out = _peer_dense(h1b, u_b, v_b, t1, e1, e2, tm_big)
    y = _out_stage(h1, peer_out, p_all, row(ln2_g[0]), row(ln2_b[0]), wg, row(b_pg[0]), we, alpha, tm_mid)

    k_p = z[:np_tok, w_fox:2 * w_fox].reshape(1, batch, seq, n_fox, HEAD_DIM)
    v_p = z[:np_tok, 2 * w_fox:3 * w_fox].reshape(1, batch, seq, n_fox, HEAD_DIM)
    k_s = z[np_tok:, w_fox:2 * w_fox].reshape(1, db, t_dec, n_fox, HEAD_DIM)
    v_s = z[np_tok:, 2 * w_fox:3 * w_fox].reshape(1, db, t_dec, n_fox, HEAD_DIM)
    logf = logf_t.T
    return (y[:np_tok].reshape(batch, seq, d_model), y[np_tok:].reshape(db, t_dec, d_model),
            k_p, v_p, logf[:np_tok].reshape(1, batch, seq, n_fox), s_p[None],
            k_s, v_s, logf[np_tok:].reshape(1, db, t_dec, n_fox), s_s[None])
```

```python
import functools
import math

import numpy as np
import jax
import jax.numpy as jnp
from jax import lax
from jax.experimental import pallas as pl
from jax.experimental.pallas import tpu as pltpu

F32 = jnp.float32
BF16 = jnp.bfloat16

HEAD_DIM = 128
PAGE_SIZE = 128
ROPE_BASE = 10000.0
RET_CHUNK = 128
PEER_HEADS = 8
PEER_KEYS = 128
PEER_HALF = 128
PEER_TOPK = 16
LN_EPS = 1e-5
GN_EPS = 1e-6
NEG_INF = float("-inf")
BIG = 1e30

VMEM_LIMIT = 56 * 1024 * 1024

_NT = (((1,), (1,)), ((), ()))
_TN = (((0,), (0,)), ((), ()))


def _pick(n, candidates):
    for c in candidates:
        if n % c == 0:
            return c
    raise ValueError(f"no tile in {candidates} divides {n}")


def _idiv(x, c):
    assert c & (c - 1) == 0
    return lax.shift_right_logical(x, int(math.log2(c)))


def _imod(x, c):
    assert c & (c - 1) == 0
    return x & (c - 1)


def _params(*sem):
    return pltpu.CompilerParams(dimension_semantics=sem, vmem_limit_bytes=VMEM_LIMIT)


def _in_proj_body(x_ref, w_ref, wf_ref, b_ref, q_ref, k_ref, v_ref, zr_ref, lf_ref):
    j = pl.program_id(1)
    xb = x_ref[...].astype(BF16)
    r = jnp.dot(xb, w_ref[...], preferred_element_type=F32)

    @pl.when(j == 0)
    def _():
        q_ref[...] = r
        f = lax.dot_general(wf_ref[...], xb, _NT, preferred_element_type=F32) + b_ref[...]
        lf_ref[...] = jnp.minimum(f, 0.0) - jnp.log1p(jnp.exp(-jnp.abs(f)))

    @pl.when(j == 1)
    def _():
        k_ref[...] = r

    @pl.when(j == 2)
    def _():
        v_ref[...] = r

    @pl.when(j >= 3)
    def _():
        zr_ref[...] = r


def _in_proj(x, w_main, wf_t, b_col, w_fox):
    m, d = x.shape
    n_cols = w_main.shape[1]
    n_h = wf_t.shape[0]
    tm = _pick(m, (512, 256, 128))
    nj = n_cols // w_fox
    assert nj * w_fox == n_cols and nj > 3
    head = lambda: pl.BlockSpec((tm, w_fox), lambda i, j: (i, 0))
    return pl.pallas_call(
        _in_proj_body,
        grid=(m // tm, nj),
        in_specs=[pl.BlockSpec((tm, d), lambda i, j: (i, 0)), pl.BlockSpec((d, w_fox), lambda i, j: (0, j)),
                  pl.BlockSpec((n_h, d), lambda i, j: (0, 0)), pl.BlockSpec((n_h, 1), lambda i, j: (0, 0))],
        out_specs=[head(), head(), head(),
                   pl.BlockSpec((tm, w_fox), lambda i, j: (i, jnp.maximum(j - 3, 0))),
                   pl.BlockSpec((n_h, tm), lambda i, j: (0, i))],
        out_shape=[jax.ShapeDtypeStruct((m, w_fox), F32)] * 3
        + [jax.ShapeDtypeStruct((m, n_cols - 3 * w_fox), F32), jax.ShapeDtypeStruct((n_h, m), F32)],
        compiler_params=_params("parallel", "arbitrary"),
        name="in_proj",
    )(x, w_main, wf_t, b_col)


def _split3(x):
    hi = x.astype(BF16)
    r1 = x - hi.astype(F32)
    mid = r1.astype(BF16)
    lo = (r1 - mid.astype(F32)).astype(BF16)
    return hi, mid, lo


def _dot3(parts, m):
    acc = jnp.dot(parts[0], m, preferred_element_type=F32)
    acc = acc + jnp.dot(parts[1], m, preferred_element_type=F32)
    return acc + jnp.dot(parts[2], m, preferred_element_type=F32)


def _cumsum_body(x_ref, o_ref, *, seg, blk):
    rows, length = x_ref.shape
    r = lax.broadcasted_iota(jnp.int32, (blk, blk), 0)
    c = lax.broadcasted_iota(jnp.int32, (blk, blk), 1)
    if seg >= blk:
        tri = (r <= c)
    else:
        tri = (r <= c) & (_idiv(r, seg) == _idiv(c, seg))
    tri = jnp.where(tri, 1.0, 0.0).astype(BF16)
    carry = jnp.zeros((rows, 1), F32)
    for j in range(length // blk):
        if seg < blk or (j * blk) % seg == 0:
            carry = jnp.zeros((rows, 1), F32)
        cs = _dot3(_split3(x_ref[:, j * blk:(j + 1) * blk]), tri) + carry
        o_ref[:, j * blk:(j + 1) * blk] = cs
        carry = cs[:, blk - 1:blk]


def _segment_cumsum(x, seg):
    rows, length = x.shape
    blk = _pick(length, (256, 128))
    assert seg % blk == 0 or blk % seg == 0
    return pl.pallas_call(
        functools.partial(_cumsum_body, seg=seg, blk=blk),
        out_shape=jax.ShapeDtypeStruct((rows, length), F32),
        compiler_params=pltpu.CompilerParams(vmem_limit_bytes=VMEM_LIMIT),
        name="segment_cumsum",
    )(x)


def _fox_prompt_body(q_ref, k_ref, v_ref, c_ref, o_ref, *, tq, scale):
    s_len = q_ref.shape[0]
    n_h = q_ref.shape[1] // HEAD_DIM
    row = lax.broadcasted_iota(jnp.int32, (tq, tq), 0)
    col = lax.broadcasted_iota(jnp.int32, (tq, tq), 1)
    causal = col <= row

    def step(qs, kj, carry, mask):
        ks = kj * tq if isinstance(kj, int) else pl.multiple_of(kj * tq, tq)
        out = []
        for h in range(n_h):
            m, l, acc = carry[h]
            lanes = slice(h * HEAD_DIM, (h + 1) * HEAD_DIM)
            k = k_ref[pl.ds(ks, tq), lanes].astype(BF16)
            v = v_ref[pl.ds(ks, tq), lanes].astype(BF16)
            s = lax.dot_general(qs[h], k, _NT, preferred_element_type=F32) - c_ref[h, kj]
            if mask:
                s = jnp.where(causal, s, NEG_INF)
            m_new = jnp.maximum(m, jnp.max(s, axis=1, keepdims=True))
            p = jnp.exp(s - m_new)
            alpha = jnp.exp(m - m_new)
            l = alpha * l + jnp.sum(p, axis=1, keepdims=True)
            acc = alpha * acc + jnp.dot(p.astype(BF16), v, preferred_element_type=F32)
            out.append((m_new, l, acc))
        return tuple(out)

    for qi in range(s_len // tq):
        qs = [(q_ref[qi * tq:(qi + 1) * tq, h * HEAD_DIM:(h + 1) * HEAD_DIM] * scale).astype(BF16)
              for h in range(n_h)]
        carry = tuple((jnp.full((tq, 1), NEG_INF, F32), jnp.zeros((tq, 1), F32), jnp.zeros((tq, HEAD_DIM), F32))
                      for _ in range(n_h))
        if qi > 0:
            carry = lax.fori_loop(0, qi, lambda kj, cr: step(qs, kj, cr, False), carry)
        carry = step(qs, qi, carry, True)
        for h in range(n_h):
            m, l, acc = carry[h]
            o_ref[qi * tq:(qi + 1) * tq, h * HEAD_DIM:(h + 1) * HEAD_DIM] = (acc / l).astype(o_ref.dtype)


FOX_HEADS_PER_STEP = 4


def _fox_prompt(q, k, v, c_t, batch, seq, n_fox):
    tq = _pick(seq, (256, 128))
    hp = FOX_HEADS_PER_STEP
    assert n_fox % hp == 0
    c4 = c_t.reshape(n_fox, batch * seq // tq, 1, tq)
    nb = seq // tq
    groups = n_fox // hp
    width = hp * HEAD_DIM
    return pl.pallas_call(
        functools.partial(_fox_prompt_body, tq=tq, scale=HEAD_DIM ** -0.5),
        grid=(batch, groups),
        in_specs=[
            pl.BlockSpec((seq, width), lambda b, h: (b, h)),
            pl.BlockSpec((seq, width), lambda b, h: (b, h)),
            pl.BlockSpec((seq, width), lambda b, h: (b, h)),
            pl.BlockSpec((hp, nb, 1, tq), lambda b, h: (h, b, 0, 0)),
        ],
        out_specs=pl.BlockSpec((seq, width), lambda b, h: (b, h)),
        out_shape=jax.ShapeDtypeStruct((batch * seq, n_fox * HEAD_DIM), BF16),
        compiler_params=_params("parallel", "parallel"),
        name="fox_prompt",
    )(q, k, v, c4)


def _page_suffix_body(lp_ref, et_ref):
    n = lp_ref.shape[1]
    n_h = n // PAGE_SIZE
    r = lax.broadcasted_iota(jnp.int32, (n, n), 0)
    c = lax.broadcasted_iota(jnp.int32, (n, n), 1)
    same_head = _imod(r, n_h) == _imod(c, n_h)
    later = _idiv(r, n_h) > _idiv(c, n_h)
    m_e = jnp.where(same_head & later, 1.0, 0.0).astype(BF16)
    m_t = jnp.where(same_head, 1.0, 0.0).astype(BF16)
    parts = _split3(lp_ref[...])
    et_ref[:, 0, :] = _dot3(parts, m_e)
    et_ref[:, 1, :] = _dot3(parts, m_t)


def _page_suffix(lp_rows):
    n_pool, n = lp_rows.shape
    tr = _pick(n_pool, (512, 256, 128, 64, 32, 16, 8))
    return pl.pallas_call(
        _page_suffix_body,
        grid=(n_pool // tr,),
        in_specs=[pl.BlockSpec((tr, n), lambda i: (i, 0))],
        out_specs=pl.BlockSpec((tr, 2, n), lambda i: (i, 0, 0)),
        out_shape=jax.ShapeDtypeStruct((n_pool, 2, n), F32),
        compiler_params=_params("parallel"),
        name="page_suffix",
    )(lp_rows)


def _rows_per_head(x, t):
    h, l = x.shape
    return jnp.broadcast_to(x[:, None, :], (h, t, l)).reshape(h * t, l)


def _fox_sample_body(pt_ref, q_ref, kn_ref, vn_ref, cn_ref, *refs, scale, n_fox, n_grp):
    k_refs = refs[:n_grp]
    v_refs = refs[n_grp:2 * n_grp]
    et_refs = refs[2 * n_grp:3 * n_grp]
    o_ref, q_sc, hm_sc, m_sc, l_sc, acc, carry = refs[3 * n_grp:]
    p = pl.program_id(1)
    t = q_ref.shape[0]
    rows = n_fox * t
    keys = PAGE_SIZE * n_fox

    @pl.when(p == 0)
    def _init():
        q_sc[...] = (jnp.concatenate([q_ref[:, h * HEAD_DIM:(h + 1) * HEAD_DIM] for h in range(n_fox)], axis=0)
                     * scale).astype(BF16)
        r = lax.broadcasted_iota(jnp.int32, (rows, keys), 0)
        c = lax.broadcasted_iota(jnp.int32, (rows, keys), 1)
        hm_sc[...] = jnp.where(_idiv(r, t) == _imod(c, n_fox), 0.0, NEG_INF)
        m_sc[...] = jnp.full(m_sc.shape, NEG_INF, F32)
        l_sc[...] = jnp.zeros(l_sc.shape, F32)
        acc[...] = jnp.zeros(acc.shape, F32)
        carry[...] = jnp.zeros(carry.shape, F32)

    q = q_sc[...]
    hm = hm_sc[...]
    c = carry[...]
    scores = []
    for g in range(n_grp):
        et = et_refs[g][...]
        kf = k_refs[g][...].reshape(keys, HEAD_DIM).astype(BF16)
        scores.append(lax.dot_general(q, kf, _NT, preferred_element_type=F32) + (et[0:1] + c) + hm)
        c = c + et[1:2]
    carry[...] = c
    m_old = m_sc[...]
    m_new = m_old
    for s in scores:
        m_new = jnp.maximum(m_new, jnp.max(s, axis=1, keepdims=True))
    alpha = jnp.exp(m_old - m_new)
    l_new = alpha * l_sc[...]
    acc_new = alpha * acc[...]
    for g in range(n_grp):
        pr = jnp.exp(scores[g] - m_new)
        l_new = l_new + jnp.sum(pr, axis=1, keepdims=True)
        vf = v_refs[g][...].reshape(keys, HEAD_DIM).astype(BF16)
        acc_new = acc_new + jnp.dot(pr.astype(BF16), vf, preferred_element_type=F32)
    m_sc[...] = m_new
    l_sc[...] = l_new
    acc[...] = acc_new

    @pl.when(p == pl.num_programs(1) - 1)
    def _fin():
        width = n_fox * HEAD_DIM
        r = lax.broadcasted_iota(jnp.int32, (rows, width), 0)
        cw = lax.broadcasted_iota(jnp.int32, (rows, width), 1)
        q_rep = jnp.concatenate([q_ref[...]] * n_fox, axis=0) * scale
        qbd = jnp.where(_idiv(r, t) == _idiv(cw, HEAD_DIM), q_rep, 0.0).astype(BF16)
        pad = jnp.zeros((PAGE_SIZE - t, width), BF16)
        k_new = jnp.concatenate([kn_ref[...].astype(BF16), pad], axis=0)
        v_new = jnp.concatenate([vn_ref[...].astype(BF16), pad], axis=0)
        s = lax.dot_general(qbd, k_new, _NT, preferred_element_type=F32) - _rows_per_head(cn_ref[...], t)
        rr = lax.broadcasted_iota(jnp.int32, (rows, PAGE_SIZE), 0)
        cc = lax.broadcasted_iota(jnp.int32, (rows, PAGE_SIZE), 1)
        s = jnp.where(cc <= _imod(rr, t), s, NEG_INF)
        m_fin = jnp.maximum(m_new, jnp.max(s, axis=1, keepdims=True))
        pr = jnp.exp(s - m_fin)
        a_fin = jnp.exp(m_new - m_fin)
        l_fin = a_fin * l_new + jnp.sum(pr, axis=1, keepdims=True)
        o_wide = jnp.dot(pr.astype(BF16), v_new, preferred_element_type=F32)
        for h in range(n_fox):
            blk = slice(h * t, (h + 1) * t)
            lanes = slice(h * HEAD_DIM, (h + 1) * HEAD_DIM)
            o_h = a_fin[blk] * acc_new[blk] + o_wide[blk, lanes]
            o_ref[:, lanes] = (o_h / l_fin[blk]).astype(o_ref.dtype)


def _fox_sample(q, k_new, v_new, cn_pad, cache_k, cache_v, page_et, page_table, t, n_fox):
    db, n_pages = page_table.shape
    width = n_fox * HEAD_DIM
    keys = PAGE_SIZE * n_fox
    n_grp = _pick(n_pages, (16, 8, 4, 2, 1))
    last = n_pages - 1

    def kv_spec(g):
        return pl.BlockSpec((None, None, PAGE_SIZE, n_fox, HEAD_DIM),
                            lambda b, p, pt: (0, pt[b, last - p * n_grp - g], 0, 0, 0))

    def et_spec(g):
        return pl.BlockSpec((None, 2, keys), lambda b, p, pt: (pt[b, last - p * n_grp - g], 0, 0))

    rng = range(n_grp)
    grid_spec = pltpu.PrefetchScalarGridSpec(
        num_scalar_prefetch=1,
        grid=(db, n_pages // n_grp),
        in_specs=[
            pl.BlockSpec((t, width), lambda b, p, pt: (b, 0)),
            pl.BlockSpec((t, width), lambda b, p, pt: (b, 0)),
            pl.BlockSpec((t, width), lambda b, p, pt: (b, 0)),
            pl.BlockSpec((None, n_fox, PAGE_SIZE), lambda b, p, pt: (b, 0, 0)),
        ] + [kv_spec(g) for g in rng] + [kv_spec(g) for g in rng] + [et_spec(g) for g in rng],
        out_specs=pl.BlockSpec((t, width), lambda b, p, pt: (b, 0)),
        scratch_shapes=[
            pltpu.VMEM((n_fox * t, HEAD_DIM), BF16),
            pltpu.VMEM((n_fox * t, keys), F32),
            pltpu.VMEM((n_fox * t, 1), F32),
            pltpu.VMEM((n_fox * t, 1), F32),
            pltpu.VMEM((n_fox * t, HEAD_DIM), F32),
            pltpu.VMEM((1, keys), F32),
        ],
    )
    return pl.pallas_call(
        functools.partial(_fox_sample_body, scale=HEAD_DIM ** -0.5, n_fox=n_fox, n_grp=n_grp),
        grid_spec=grid_spec,
        out_shape=jax.ShapeDtypeStruct((db * t, width), F32),
        compiler_params=_params("parallel", "arbitrary"),
        name="fox_sample",
    )(page_table, q, k_new, v_new, cn_pad, *([cache_k] * n_grp), *([cache_v] * n_grp), *([page_et] * n_grp))


def _rotate(x, cos_f, sin_f):
    return x * cos_f + pltpu.roll(x, HEAD_DIM // 2, 1) * sin_f


def _retention_body(q_ref, k_ref, v_ref, g_ref, cos_ref, sin_ref, dec_ref, qd_ref, kd_ref, cd_ref, gn_ref,
                    s0_ref, o_ref, sout_ref, state, *, k_scale):
    c = pl.program_id(1)
    n_ret = state.shape[0]

    @pl.when(c == 0)
    def _():
        state[...] = s0_ref[...]

    cos_f = cos_ref[...]
    sin_f = sin_ref[...]
    for h in range(n_ret):
        lanes = slice(h * HEAD_DIM, (h + 1) * HEAD_DIM)
        q = _rotate(q_ref[:, lanes], cos_f, sin_f)
        k = _rotate(k_ref[:, lanes], cos_f, sin_f) * k_scale
        qb = q.astype(BF16)
        vb = v_ref[:, lanes].astype(BF16)
        s_in = state[h]
        scores = lax.dot_general(qb, k.astype(BF16), _NT, preferred_element_type=F32) * dec_ref[h]
        y = jnp.dot(scores.astype(BF16), vb, preferred_element_type=F32)
        y = y + jnp.dot(qb, s_in.astype(BF16), preferred_element_type=F32) * qd_ref[h]
        kd = (k * kd_ref[h]).astype(BF16)
        new_state = cd_ref[h] * s_in + lax.dot_general(kd, vb, _TN, preferred_element_type=F32)
        state[h] = new_state
        mu = jnp.mean(y, axis=1, keepdims=True)
        yc = y - mu
        var = jnp.mean(yc * yc, axis=1, keepdims=True)
        yn = yc * lax.rsqrt(var + GN_EPS) * gn_ref[:, lanes]
        g = g_ref[:, lanes]
        o_ref[:, lanes] = (g / (1.0 + jnp.exp(-g)) * yn).astype(o_ref.dtype)

    @pl.when(c == pl.num_programs(1) - 1)
    def _():
        sout_ref[...] = state[...]


def _retention(z, row0, col0, n_ret, batch, seq, chunk, state0, cos_f, sin_f, tables, gn_row):
    dec, qd, kd, cd = tables
    nc = seq // chunk
    rb0 = row0 // chunk
    d = HEAD_DIM
    width = n_ret * d
    assert col0 % width == 0
    cb0 = col0 // width
    zspec = lambda off: pl.BlockSpec((chunk, width), lambda b, c: (rb0 + b * nc + c, cb0 + off))
    full = lambda a: pl.BlockSpec(a.shape, lambda b, c: (0,) * a.ndim)
    out, s_fin = pl.pallas_call(
        functools.partial(_retention_body, k_scale=HEAD_DIM ** -0.5),
        grid=(batch, nc),
        in_specs=[
            zspec(0), zspec(1), zspec(2), zspec(3),
            pl.BlockSpec((chunk, d), lambda b, c: (c, 0)),
            pl.BlockSpec((chunk, d), lambda b, c: (c, 0)),
            full(dec), full(qd), full(kd), full(cd), full(gn_row),
            pl.BlockSpec((None, n_ret, d, d), lambda b, c: (b, 0, 0, 0)),
        ],
        out_specs=[
            pl.BlockSpec((chunk, width), lambda b, c: (b * nc + c, 0)),
            pl.BlockSpec((None, n_ret, d, d), lambda b, c: (b, 0, 0, 0)),
        ],
        out_shape=[
            jax.ShapeDtypeStruct((batch * seq, width), BF16 if chunk % 16 == 0 else F32),
            jax.ShapeDtypeStruct((batch, n_ret, d, d), F32),
        ],
        scratch_shapes=[pltpu.VMEM((n_ret, d, d), F32)],
        compiler_params=_params("parallel", "arbitrary"),
        name=f"retention_{chunk}",
    )(z, z, z, z, cos_f, sin_f, dec, qd, kd, cd, gn_row, state0)
    return out, s_fin


def _retention_tables(n_ret, chunk):
    log_gamma = jnp.log1p(-jnp.exp2(-5.0 - jnp.arange(n_ret, dtype=F32)))
    pos = jnp.arange(chunk, dtype=F32)
    diff = pos[:, None] - pos[None, :]
    dec = jnp.where((diff >= 0)[None], jnp.exp(jnp.maximum(diff, 0.0)[None] * log_gamma[:, None, None]), 0.0)
    qd = jnp.exp((pos[None, :] + 1.0) * log_gamma[:, None])
    kd = jnp.exp((chunk - 1.0 - pos)[None, :] * log_gamma[:, None])
    cd = jnp.exp(chunk * log_gamma)
    bc = lambda a: jnp.broadcast_to(a[:, :, None], (n_ret, chunk, HEAD_DIM))
    return dec, bc(qd), bc(kd), jnp.broadcast_to(cd[:, None, None], (n_ret, 1, HEAD_DIM))


def _rotary_tables(pos):
    half = HEAD_DIM // 2
    inv_freq = ROPE_BASE ** (-jnp.arange(half, dtype=F32) / half)
    ang = pos[:, None] * inv_freq[None, :]
    cos, sin = jnp.cos(ang), jnp.sin(ang)
    return jnp.concatenate([cos, cos], -1), jnp.concatenate([-sin, sin], -1)


def _layer_norm(x, g, b):
    mu = jnp.mean(x, axis=1, keepdims=True)
    xc = x - mu
    var = jnp.mean(xc * xc, axis=1, keepdims=True)
    return xc * lax.rsqrt(var + LN_EPS) * g + b


def _group_specs(tm, width, tiles_p):
    return (pl.BlockSpec((tm, width), lambda i: (jnp.minimum(i, tiles_p - 1), 0)),
            pl.BlockSpec((tm, width), lambda i: (jnp.maximum(i - tiles_p, 0), 0)))


def _wo_ln_body(xp_ref, xs_ref, ofp_ref, ofs_ref, orp_ref, ors_ref, w1_ref, w2_ref, g_ref, b_ref, h_ref, hb_ref,
                *, alpha, tiles_p):
    is_p = pl.program_id(0) < tiles_p
    x = jnp.where(is_p, xp_ref[...], xs_ref[...])
    o_fox = jnp.where(is_p, ofp_ref[...], ofs_ref[...].astype(BF16))
    o_ret = jnp.where(is_p, orp_ref[...], ors_ref[...].astype(BF16))
    mixed = jnp.dot(o_fox, w1_ref[...], preferred_element_type=F32)
    mixed = mixed + jnp.dot(o_ret, w2_ref[...], preferred_element_type=F32)
    h = _layer_norm(alpha * x + mixed, g_ref[...], b_ref[...])
    h_ref[...] = h
    hb_ref[...] = h.astype(BF16)


def _wo_ln(x_p, x_s, of_p, of_s, or_p, or_s, w1, w2, g, b, alpha, tm):
    d = x_p.shape[1]
    tiles_p, tiles_s = x_p.shape[0] // tm, x_s.shape[0] // tm
    n = x_p.shape[0] + x_s.shape[0]
    row = lambda w: pl.BlockSpec((tm, w), lambda i: (i, 0))
    full = lambda a: pl.BlockSpec(a.shape, lambda i: (0, 0))
    return pl.pallas_call(
        functools.partial(_wo_ln_body, alpha=alpha, tiles_p=tiles_p),
        grid=(tiles_p + tiles_s,),
        in_specs=[*_group_specs(tm, d, tiles_p), *_group_specs(tm, of_p.shape[1], tiles_p),
                  *_group_specs(tm, or_p.shape[1], tiles_p), full(w1), full(w2), full(g), full(b)],
        out_specs=[row(d), row(d)],
        out_shape=[jax.ShapeDtypeStruct((n, d), F32), jax.ShapeDtypeStruct((n, d), BF16)],
        compiler_params=_params("arbitrary"),
        name="wo_ln1",
    )(x_p, x_s, of_p, of_s, or_p, or_s, w1, w2, g, b)


def _top_desc(s, count, n_masked, one_per_round):
    n = s.shape[0]

    vals, work = [], s
    if one_per_round:
        iota = lax.broadcasted_iota(jnp.int32, s.shape, 0)
        for _ in range(count):
            mj = jnp.max(work, axis=0, keepdims=True)
            vals.append(mj)
            first = jnp.min(jnp.where(work == mj, iota, n), axis=0, keepdims=True)
            work = jnp.where(iota == first, NEG_INF, work)
        return vals, jnp.zeros((1, s.shape[1]), F32)
    for _ in range(count):
        mj = jnp.max(work, axis=0, keepdims=True)
        vals.append(mj)
        work = jnp.where(work == mj, NEG_INF, work)
    removed = jnp.sum(jnp.where(work == NEG_INF, 1.0, 0.0), axis=0, keepdims=True)
    return vals, jnp.abs(removed - float(count + n_masked))


_CAND = [(a, b) for a in range(PEER_TOPK + 1) for b in range(PEER_TOPK + 1) if (a + 1) * (b + 1) <= PEER_TOPK + 1]


def _peer_score_body(hb_ref, wq_ref, keys_ref, t1_ref, e1_ref, e2_ref, q_sc):
    tm = hb_ref.shape[0]
    q_sc[...] = jnp.dot(hb_ref[...], wq_ref[...], preferred_element_type=F32).astype(BF16)

    def head(h, one_per_round):
        s = []
        for c in range(2):
            col = (2 * h + c) * PEER_HALF
            s.append(lax.dot_general(keys_ref[h, c], q_sc[:, col:col + PEER_HALF], _NT,
                                     preferred_element_type=F32))
        v1, bad1 = _top_desc(s[0], PEER_TOPK + 1, 0, one_per_round)
        v2, bad2 = _top_desc(s[1], PEER_TOPK + 1, 0, one_per_round)
        rows = [v1[a] + v2[b] for a, b in _CAND]
        pad = (-len(rows)) % 8
        cand = jnp.concatenate(rows + [jnp.full((pad, tm), NEG_INF, F32)], axis=0)
        top, bad3 = _top_desc(cand, PEER_TOPK + 1, pad, one_per_round)
        z = jnp.ones((1, tm), F32)
        for j in range(1, PEER_TOPK):
            z = z + jnp.exp(top[j] - top[0])
        theta = 0.5 * (top[PEER_TOPK - 1] + top[PEER_TOPK])
        c1 = s[0] - v1[0]
        t1_ref[h] = jnp.exp((theta - top[0]) - c1)
        e1_ref[h] = jnp.exp(c1 - jnp.log(z))
        e2_ref[h] = jnp.exp(s[1] - v2[0])
        return jnp.max(bad1 + bad2 + bad3)

    bad = [head(h, False) for h in range(PEER_HEADS)]
    for h in range(PEER_HEADS):
        @pl.when(bad[h] > 0.0)
        def _():
            head(h, True)


def _peer_scores(hb, wq, keys, tm):
    n, d = hb.shape
    return pl.pallas_call(
        _peer_score_body,
        grid=(n // tm,),
        in_specs=[pl.BlockSpec((tm, d), lambda i: (i, 0)), pl.BlockSpec(wq.shape, lambda i: (0, 0)),
                  pl.BlockSpec(keys.shape, lambda i: (0, 0, 0, 0))],
        out_specs=[pl.BlockSpec((PEER_HEADS, PEER_KEYS, tm), lambda i: (0, 0, i))] * 3,
        out_shape=[jax.ShapeDtypeStruct((PEER_HEADS, PEER_KEYS, n), F32)] * 3,
        scratch_shapes=[pltpu.VMEM((tm, wq.shape[1]), BF16)],
        compiler_params=_params("parallel"),
        name="peer_scores",
    )(hb, wq, keys)


PEER_SUB = 256


PEER_EB = 2 * PEER_SUB


def _peer_dense_body(hb_ref, u_ref, v0_ref, vp_ref, t1_ref, e1_ref, e2_ref, o_ref, act_sc, w_sc, acc_sc):
    j = pl.program_id(1)
    n_steps = pl.num_programs(1) - 1
    tm = hb_ref.shape[0]
    n_i1 = PEER_SUB // PEER_KEYS

    @pl.when(j == 0)
    def _():
        o_ref[...] = jnp.zeros(o_ref.shape, F32)
        acc_sc[...] = jnp.zeros(acc_sc.shape, F32)
        w_sc[1] = jnp.zeros(w_sc.shape[1:], BF16)

    d_model = o_ref.shape[1]
    n_tiles = tm // 128
    col_chunks = [slice(q * 256, (q + 1) * 256) for q in range(d_model // 256)]

    def product(slot, v_ref, cols):
        dst = acc_sc if slot == 1 else o_ref
        dst[:, cols] += jnp.dot(w_sc[slot], v_ref[:, cols], preferred_element_type=F32)

    def build_w(r, prev_slot, prev_v):
        act = lax.dot_general(hb_ref[...], u_ref[r * PEER_SUB:(r + 1) * PEER_SUB, :], _NT,
                              preferred_element_type=F32)
        act_sc[r] = 0.5 * act * (1.0 + lax.erf(act * (2.0 ** -0.5)))
        i1_0 = (2 * j + r) * n_i1
        t1_rows = [[t1_ref[h, pl.ds(i1_0 + k, 1), :] for k in range(n_i1)] for h in range(PEER_HEADS)]
        e1_rows = [[e1_ref[h, pl.ds(i1_0 + k, 1), :] for k in range(n_i1)] for h in range(PEER_HEADS)]
        pending_chunks = list(col_chunks)
        for c in range(n_tiles):
            lanes = slice(c * 128, (c + 1) * 128)
            gates = [None] * n_i1
            for h in range(PEER_HEADS):
                e2 = e2_ref[h, :, lanes]
                for k in range(n_i1):
                    g = jnp.where(e2 >= t1_rows[h][k][:, lanes], e2, 0.0) * e1_rows[h][k][:, lanes]
                    gates[k] = g if gates[k] is None else gates[k] + g
            for k in range(n_i1):
                cols = slice(k * PEER_KEYS, (k + 1) * PEER_KEYS)
                w_sc[r, lanes, cols] = (act_sc[r, lanes, cols] * gates[k].T).astype(BF16)
            n_now = -(-len(pending_chunks) // (n_tiles - c))
            for _ in range(n_now):
                product(prev_slot, prev_v, pending_chunks.pop(0))

    @pl.when(j < n_steps)
    def _main():
        build_w(0, 1, vp_ref)
        build_w(1, 0, v0_ref)

    @pl.when(j == n_steps)
    def _drain():
        for cols in col_chunks:
            product(1, vp_ref, cols)
        o_ref[...] += acc_sc[...]


def _peer_dense(hb, u, v, t1, e1, e2, tm):
    n, d = hb.shape
    n_steps = u.shape[0] // PEER_EB
    last = n_steps - 1
    per_tile = lambda shape, imap: pl.BlockSpec(shape, imap, pipeline_mode=pl.Buffered(1))
    gate_spec = lambda: per_tile((PEER_HEADS, PEER_KEYS, tm), lambda i, j: (0, 0, i))
    return pl.pallas_call(
        _peer_dense_body,
        grid=(n // tm, n_steps + 1),
        in_specs=[
            per_tile((tm, d), lambda i, j: (i, 0)),
            pl.BlockSpec((PEER_EB, d), lambda i, j: (jnp.minimum(j, last), 0)),
            pl.BlockSpec((PEER_SUB, d), lambda i, j: (2 * jnp.minimum(j, last), 0)),
            pl.BlockSpec((PEER_SUB, d), lambda i, j: (jnp.maximum(2 * j - 1, 0), 0)),
            gate_spec(), gate_spec(), gate_spec(),
        ],
        out_specs=pl.BlockSpec((tm, d), lambda i, j: (i, 0)),
        out_shape=jax.ShapeDtypeStruct((n, d), F32),
        scratch_shapes=[pltpu.VMEM((2, tm, PEER_SUB), F32), pltpu.VMEM((2, tm, PEER_SUB), BF16),
                        pltpu.VMEM((tm, d), F32)],
        compiler_params=_params("parallel", "arbitrary"),
        name="peer_dense",
    )(hb, u, v, v, t1, e1, e2)


def _out_body(h_ref, po_ref, pp_ref, ps_ref, g_ref, b_ref, wg_ref, bg_ref, we_ref, yp_ref, ys_ref,
              *, alpha, tiles_p):
    i = pl.program_id(0)
    h2 = _layer_norm(alpha * h_ref[...] + po_ref[...], g_ref[...], b_ref[...])
    lin = jnp.dot(h2.astype(BF16), wg_ref[...], preferred_element_type=F32) + bg_ref[...]
    p_emb = jnp.where(i < tiles_p, pp_ref[...], ps_ref[...])
    emb = jnp.dot(p_emb.astype(BF16), we_ref[...], preferred_element_type=F32)
    y = h2 + emb / (1.0 + jnp.exp(-lin))

    @pl.when(i < tiles_p)
    def _():
        yp_ref[...] = y

    @pl.when(i >= tiles_p)
    def _():
        ys_ref[...] = y


def _out_stage(h1, peer_out, p_p, p_s, g, b, wg, bg, we, alpha, tm):
    n, d = h1.shape
    tiles_p, tiles_s = p_p.shape[0] // tm, p_s.shape[0] // tm
    assert (tiles_p + tiles_s) * tm == n
    row = lambda w: pl.BlockSpec((tm, w), lambda i: (i, 0))
    full = lambda a: pl.BlockSpec(a.shape, lambda i: (0, 0))
    return pl.pallas_call(
        functools.partial(_out_body, alpha=alpha, tiles_p=tiles_p),
        grid=(tiles_p + tiles_s,),
        in_specs=[row(d), row(d), *_group_specs(tm, p_p.shape[1], tiles_p),
                  full(g), full(b), full(wg), full(bg), full(we)],
        out_specs=list(_group_specs(tm, d, tiles_p)),
        out_shape=[jax.ShapeDtypeStruct((tiles_p * tm, d), F32), jax.ShapeDtypeStruct((tiles_s * tm, d), F32)],
        compiler_params=_params("arbitrary"),
        name="ln2_gate_out",
    )(h1, peer_out, p_p, p_s, g, b, wg, bg, we)


def kernel(x_prompt, x_sample, cache_k, cache_v, cache_logf, state_ret, page_table, p_prompt, p_sample, w_in, b_f,
           gn_g, w_o, ln1_g, ln1_b, w_pq, peer_keys, peer_u, peer_v, ln2_g, ln2_b, w_pg, b_pg, w_pe):
    depth = w_in.shape[0]
    assert depth == 1, "one layer"
    batch, seq, d_model = x_prompt.shape
    db, t_dec, _ = x_sample.shape
    n_pool = cache_k.shape[1]
    n_fox = cache_k.shape[3]
    n_heads = d_model // HEAD_DIM
    n_ret = n_heads - n_fox
    w_fox = n_fox * HEAD_DIM
    w_ret = n_ret * HEAD_DIM
    n_pages = page_table.shape[1]
    past_len = n_pages * PAGE_SIZE
    np_tok = batch * seq
    ns_tok = db * t_dec
    n_tok = np_tok + ns_tok
    alpha = (2.0 * depth) ** 0.25
    assert cache_k.shape[2] == PAGE_SIZE and cache_k.shape[4] == HEAD_DIM and t_dec % 8 == 0

    tm_big = _pick(n_tok, (768, 512, 384, 256, 128))
    tm_mid = _pick(math.gcd(np_tok, ns_tok), (256, 128))

    wi = w_in[0]
    f0 = 3 * w_fox
    w_main = jnp.concatenate([wi[:, :f0], wi[:, f0 + n_fox:]], axis=1).astype(BF16)
    wf_t = wi[:, f0:f0 + n_fox].T.astype(BF16)
    wo1 = w_o[0, :w_fox].astype(BF16)
    wo2 = w_o[0, w_fox:].astype(BF16)
    wq = w_pq[0].astype(BF16)
    keys = peer_keys[0].astype(BF16)
    u_b = peer_u[0].astype(BF16)
    v_b = peer_v[0].astype(BF16)
    wg = w_pg[0].astype(BF16)
    we = w_pe[0].astype(BF16)
    row = lambda a: a.reshape(1, -1)

    x_p = x_prompt.reshape(np_tok, d_model)
    x_s = x_sample.reshape(ns_tok, d_model)
    b_col = b_f[0].reshape(n_fox, 1)

    qf_p, kf_p, vf_p, zr_p, lf_p = _in_proj(x_p, w_main, wf_t, b_col, w_fox)
    qf_s, kf_s, vf_s, zr_s, lf_s = _in_proj(x_s, w_main, wf_t, b_col, w_fox)

    c_t = _segment_cumsum(lf_p, seq)
    o_fox_p = _fox_prompt(qf_p, kf_p, vf_p, c_t, batch, seq, n_fox)

    cn_t = _segment_cumsum(jnp.pad(lf_s, ((0, 0), (0, (-ns_tok) % 128))), t_dec)[:, :ns_tok]
    cn = cn_t.reshape(n_fox, db, t_dec).transpose(1, 0, 2)
    cn_pad = jnp.pad(cn, ((0, 0), (0, 0), (0, PAGE_SIZE - t_dec)), constant_values=BIG)
    page_et = _page_suffix(cache_logf[0].reshape(n_pool, PAGE_SIZE * n_fox))
    o_fox_s = _fox_sample(qf_s, kf_s, vf_s, cn_pad, cache_k, cache_v, page_et, page_table, t_dec, n_fox)

    gn_row = row(gn_g[0])
    cos_p, sin_p = _rotary_tables(jnp.arange(seq, dtype=F32))
    cos_s, sin_s = _rotary_tables(past_len + jnp.arange(t_dec, dtype=F32))
    zeros_state = jnp.zeros((batch, n_ret, HEAD_DIM, HEAD_DIM), F32)
    o_ret_p, s_p = _retention(zr_p, 0, 0, n_ret, batch, seq, RET_CHUNK, zeros_state, cos_p, sin_p,
                              _retention_tables(n_ret, RET_CHUNK), gn_row)
    o_ret_s, s_s = _retention(zr_s, 0, 0, n_ret, db, t_dec, t_dec, state_ret[0], cos_s, sin_s,
                              _retention_tables(n_ret, t_dec), gn_row)

    h1, h1b = _wo_ln(x_p, x_s, o_fox_p, o_fox_s, o_ret_p, o_ret_s, wo1, wo2, row(ln1_g[0]), row(ln1_b[0]),
                     alpha, tm_mid)
    t1, e1, e2 = _peer_scores(h1b, wq, keys, _pick(n_tok, (128,)))
    peer_out = _peer_dense(h1b, u_b, v_b, t1, e1, e2, tm_big)
    y_p, y_s = _out_stage(h1, peer_out, p_prompt[0].reshape(np_tok, -1), p_sample[0].reshape(ns_tok, -1),
                          row(ln2_g[0]), row(ln2_b[0]), wg, row(b_pg[0]), we, alpha, tm_mid)

    heads_p = lambda a: a.reshape(1, batch, seq, n_fox, HEAD_DIM)
    heads_s = lambda a: a.reshape(1, db, t_dec, n_fox, HEAD_DIM)
    return (y_p.reshape(batch, seq, d_model), y_s.reshape(db, t_dec, d_model),
            heads_p(kf_p), heads_p(vf_p), lf_p.T.reshape(1, batch, seq, n_fox), s_p[None],
            heads_s(kf_s), heads_s(vf_s), lf_s.T.reshape(1, db, t_dec, n_fox), s_s[None])
```

```python
import functools
import math

import numpy as np
import jax
import jax.numpy as jnp
from jax import lax
from jax.experimental import pallas as pl
from jax.experimental.pallas import tpu as pltpu

F32 = jnp.float32
BF16 = jnp.bfloat16

HEAD_DIM = 128
PAGE_SIZE = 128
ROPE_BASE = 10000.0
RET_CHUNK = 128
PEER_HEADS = 8
PEER_KEYS = 128
PEER_HALF = 128
PEER_TOPK = 16
LN_EPS = 1e-5
GN_EPS = 1e-6
NEG_INF = float("-inf")
BIG = 1e30

VMEM_LIMIT = 56 * 1024 * 1024

_NT = (((1,), (1,)), ((), ()))
_TN = (((0,), (0,)), ((), ()))


def _pick(n, candidates):
    for c in candidates:
        if n % c == 0:
            return c
    raise ValueError(f"no tile in {candidates} divides {n}")


def _idiv(x, c):
    assert c & (c - 1) == 0
    return lax.shift_right_logical(x, int(math.log2(c)))


def _imod(x, c):
    assert c & (c - 1) == 0
    return x & (c - 1)


def _params(*sem):
    return pltpu.CompilerParams(dimension_semantics=sem, vmem_limit_bytes=VMEM_LIMIT)


def _in_proj_body(x_ref, w_ref, wf_ref, b_ref, q_ref, k_ref, v_ref, zr_ref, lf_ref, xb_sc):
    j = pl.program_id(1)

    @pl.when(j == 0)
    def _():
        xb_sc[...] = x_ref[...].astype(BF16)

    xb = xb_sc[...]
    r = jnp.dot(xb, w_ref[...], preferred_element_type=F32)

    @pl.when(j == 0)
    def _():
        q_ref[...] = r
        f = lax.dot_general(wf_ref[...], xb, _NT, preferred_element_type=F32) + b_ref[...]
        lf_ref[...] = jnp.minimum(f, 0.0) - jnp.log1p(jnp.exp(-jnp.abs(f)))

    @pl.when(j == 1)
    def _():
        k_ref[...] = r

    @pl.when(j == 2)
    def _():
        v_ref[...] = r

    @pl.when(j >= 3)
    def _():
        zr_ref[...] = r


def _in_proj(x, w_main, wf_t, b_col, w_fox):
    m, d = x.shape
    n_cols = w_main.shape[1]
    n_h = wf_t.shape[0]
    tm = _pick(m, (512, 256, 128))
    nj = n_cols // w_fox
    assert nj * w_fox == n_cols and nj > 3
    head = lambda: pl.BlockSpec((tm, w_fox), lambda i, j: (i, 0))
    return pl.pallas_call(
        _in_proj_body,
        grid=(m // tm, nj),
        in_specs=[pl.BlockSpec((tm, d), lambda i, j: (i, 0)), pl.BlockSpec((d, w_fox), lambda i, j: (0, j)),
                  pl.BlockSpec((n_h, d), lambda i, j: (0, 0)), pl.BlockSpec((n_h, 1), lambda i, j: (0, 0))],
        out_specs=[head(), head(), head(),
                   pl.BlockSpec((tm, w_fox), lambda i, j: (i, jnp.maximum(j - 3, 0))),
                   pl.BlockSpec((n_h, tm), lambda i, j: (0, i))],
        out_shape=[jax.ShapeDtypeStruct((m, w_fox), F32)] * 3
        + [jax.ShapeDtypeStruct((m, n_cols - 3 * w_fox), F32), jax.ShapeDtypeStruct((n_h, m), F32)],
        scratch_shapes=[pltpu.VMEM((tm, d), BF16)],
        compiler_params=_params("parallel", "arbitrary"),
        name="in_proj",
    )(x, w_main, wf_t, b_col)


def _split3(x):
    hi = x.astype(BF16)
    r1 = x - hi.astype(F32)
    mid = r1.astype(BF16)
    lo = (r1 - mid.astype(F32)).astype(BF16)
    return hi, mid, lo


def _dot3(parts, m):
    acc = jnp.dot(parts[0], m, preferred_element_type=F32)
    acc = acc + jnp.dot(parts[1], m, preferred_element_type=F32)
    return acc + jnp.dot(parts[2], m, preferred_element_type=F32)


def _cumsum_body(x_ref, o_ref, *, seg, blk):
    rows, length = x_ref.shape
    r = lax.broadcasted_iota(jnp.int32, (blk, blk), 0)
    c = lax.broadcasted_iota(jnp.int32, (blk, blk), 1)
    if seg >= blk:
        tri = (r <= c)
    else:
        tri = (r <= c) & (_idiv(r, seg) == _idiv(c, seg))
    tri = jnp.where(tri, 1.0, 0.0).astype(BF16)
    carry = jnp.zeros((rows, 1), F32)
    for j in range(length // blk):
        if seg < blk or (j * blk) % seg == 0:
            carry = jnp.zeros((rows, 1), F32)
        cs = _dot3(_split3(x_ref[:, j * blk:(j + 1) * blk]), tri) + carry
        o_ref[:, j * blk:(j + 1) * blk] = cs
        carry = cs[:, blk - 1:blk]


def _segment_cumsum(x, seg):
    rows, length = x.shape
    blk = _pick(length, (256, 128))
    assert seg % blk == 0 or blk % seg == 0
    return pl.pallas_call(
        functools.partial(_cumsum_body, seg=seg, blk=blk),
        out_shape=jax.ShapeDtypeStruct((rows, length), F32),
        compiler_params=pltpu.CompilerParams(vmem_limit_bytes=VMEM_LIMIT),
        name="segment_cumsum",
    )(x)


def _fox_prompt_body(q_ref, k_ref, v_ref, c_ref, o_ref, *, tq, scale):
    s_len = q_ref.shape[0]
    n_h = q_ref.shape[1] // HEAD_DIM
    row = lax.broadcasted_iota(jnp.int32, (tq, tq), 0)
    col = lax.broadcasted_iota(jnp.int32, (tq, tq), 1)
    causal = col <= row

    def step(qs, kj, carry, mask):
        ks = kj * tq if isinstance(kj, int) else pl.multiple_of(kj * tq, tq)
        out = []
        for h in range(n_h):
            m, l, acc = carry[h]
            lanes = slice(h * HEAD_DIM, (h + 1) * HEAD_DIM)
            k = k_ref[pl.ds(ks, tq), lanes].astype(BF16)
            v = v_ref[pl.ds(ks, tq), lanes].astype(BF16)
            s = lax.dot_general(qs[h], k, _NT, preferred_element_type=F32) - c_ref[h, kj]
            if mask:
                s = jnp.where(causal, s, NEG_INF)
            m_new = jnp.maximum(m, jnp.max(s, axis=1, keepdims=True))
            p = jnp.exp(s - m_new)
            alpha = jnp.exp(m - m_new)
            l = alpha * l + jnp.sum(p, axis=1, keepdims=True)
            acc = alpha * acc + jnp.dot(p.astype(BF16), v, preferred_element_type=F32)
            out.append((m_new, l, acc))
        return tuple(out)

    for qi in range(s_len // tq):
        qs = [(q_ref[qi * tq:(qi + 1) * tq, h * HEAD_DIM:(h + 1) * HEAD_DIM] * scale).astype(BF16)
              for h in range(n_h)]
        carry = tuple((jnp.full((tq, 1), NEG_INF, F32), jnp.zeros((tq, 1), F32), jnp.zeros((tq, HEAD_DIM), F32))
                      for _ in range(n_h))
        if qi > 0:
            carry = lax.fori_loop(0, qi, lambda kj, cr: step(qs, kj, cr, False), carry)
        carry = step(qs, qi, carry, True)
        for h in range(n_h):
            m, l, acc = carry[h]
            o_ref[qi * tq:(qi + 1) * tq, h * HEAD_DIM:(h + 1) * HEAD_DIM] = (acc / l).astype(o_ref.dtype)


FOX_HEADS_PER_STEP = 4


def _fox_prompt(q, k, v, c_t, batch, seq, n_fox):
    tq = _pick(seq, (256, 128))
    hp = FOX_HEADS_PER_STEP
    assert n_fox % hp == 0
    c4 = c_t.reshape(n_fox, batch * seq // tq, 1, tq)
    nb = seq // tq
    groups = n_fox // hp
    width = hp * HEAD_DIM
    return pl.pallas_call(
        functools.partial(_fox_prompt_body, tq=tq, scale=HEAD_DIM ** -0.5),
        grid=(batch, groups),
        in_specs=[
            pl.BlockSpec((seq, width), lambda b, h: (b, h)),
            pl.BlockSpec((seq, width), lambda b, h: (b, h)),
            pl.BlockSpec((seq, width), lambda b, h: (b, h)),
            pl.BlockSpec((hp, nb, 1, tq), lambda b, h: (h, b, 0, 0)),
        ],
        out_specs=pl.BlockSpec((seq, width), lambda b, h: (b, h)),
        out_shape=jax.ShapeDtypeStruct((batch * seq, n_fox * HEAD_DIM), BF16),
        compiler_params=_params("parallel", "parallel"),
        name="fox_prompt",
    )(q, k, v, c4)


def _page_suffix_body(lp_ref, et_ref):
    n = lp_ref.shape[1]
    n_h = n // PAGE_SIZE
    r = lax.broadcasted_iota(jnp.int32, (n, n), 0)
    c = lax.broadcasted_iota(jnp.int32, (n, n), 1)
    same_head = _imod(r, n_h) == _imod(c, n_h)
    later = _idiv(r, n_h) > _idiv(c, n_h)
    m_e = jnp.where(same_head & later, 1.0, 0.0).astype(BF16)
    m_t = jnp.where(same_head, 1.0, 0.0).astype(BF16)
    parts = _split3(lp_ref[...])
    et_ref[:, 0, :] = _dot3(parts, m_e)
    et_ref[:, 1, :] = _dot3(parts, m_t)


def _page_suffix(lp_rows):
    n_pool, n = lp_rows.shape
    tr = _pick(n_pool, (512, 256, 128, 64, 32, 16, 8))
    return pl.pallas_call(
        _page_suffix_body,
        grid=(n_pool // tr,),
        in_specs=[pl.BlockSpec((tr, n), lambda i: (i, 0))],
        out_specs=pl.BlockSpec((tr, 2, n), lambda i: (i, 0, 0)),
        out_shape=jax.ShapeDtypeStruct((n_pool, 2, n), F32),
        compiler_params=_params("parallel"),
        name="page_suffix",
    )(lp_rows)


def _rows_per_head(x, t):
    h, l = x.shape
    return jnp.broadcast_to(x[:, None, :], (h, t, l)).reshape(h * t, l)


def _fox_sample_body(pt_ref, q_ref, kn_ref, vn_ref, cn_ref, *refs, scale, n_fox, n_grp):
    k_refs = refs[:n_grp]
    v_refs = refs[n_grp:2 * n_grp]
    et_refs = refs[2 * n_grp:3 * n_grp]
    o_ref, q_sc, hm_sc, m_sc, l_sc, acc, carry = refs[3 * n_grp:]
    p = pl.program_id(1)
    t = q_ref.shape[0]
    rows = n_fox * t
    keys = PAGE_SIZE * n_fox

    @pl.when(p == 0)
    def _init():
        q_sc[...] = (jnp.concatenate([q_ref[:, h * HEAD_DIM:(h + 1) * HEAD_DIM] for h in range(n_fox)], axis=0)
                     * scale).astype(BF16)
        r = lax.broadcasted_iota(jnp.int32, (rows, keys), 0)
        c = lax.broadcasted_iota(jnp.int32, (rows, keys), 1)
        hm_sc[...] = jnp.where(_idiv(r, t) == _imod(c, n_fox), 0.0, NEG_INF)
        m_sc[...] = jnp.full(m_sc.shape, NEG_INF, F32)
        l_sc[...] = jnp.zeros(l_sc.shape, F32)
        acc[...] = jnp.zeros(acc.shape, F32)
        carry[...] = jnp.zeros(carry.shape, F32)

    q = q_sc[...]
    hm = hm_sc[...]
    c = carry[...]
    scores = []
    for g in range(n_grp):
        et = et_refs[g][...]
        kf = k_refs[g][...].reshape(keys, HEAD_DIM).astype(BF16)
        scores.append(lax.dot_general(q, kf, _NT, preferred_element_type=F32) + (et[0:1] + c) + hm)
        c = c + et[1:2]
    carry[...] = c
    m_old = m_sc[...]
    m_new = m_old
    for s in scores:
        m_new = jnp.maximum(m_new, jnp.max(s, axis=1, keepdims=True))
    alpha = jnp.exp(m_old - m_new)
    l_new = alpha * l_sc[...]
    acc_new = alpha * acc[...]
    for g in range(n_grp):
        pr = jnp.exp(scores[g] - m_new)
        l_new = l_new + jnp.sum(pr, axis=1, keepdims=True)
        vf = v_refs[g][...].reshape(keys, HEAD_DIM).astype(BF16)
        acc_new = acc_new + jnp.dot(pr.astype(BF16), vf, preferred_element_type=F32)
    m_sc[...] = m_new
    l_sc[...] = l_new
    acc[...] = acc_new

    @pl.when(p == pl.num_programs(1) - 1)
    def _fin():
        width = n_fox * HEAD_DIM
        r = lax.broadcasted_iota(jnp.int32, (rows, width), 0)
        cw = lax.broadcasted_iota(jnp.int32, (rows, width), 1)
        q_rep = jnp.concatenate([q_ref[...]] * n_fox, axis=0) * scale
        qbd = jnp.where(_idiv(r, t) == _idiv(cw, HEAD_DIM), q_rep, 0.0).astype(BF16)
        pad = jnp.zeros((PAGE_SIZE - t, width), BF16)
        k_new = jnp.concatenate([kn_ref[...].astype(BF16), pad], axis=0)
        v_new = jnp.concatenate([vn_ref[...].astype(BF16), pad], axis=0)
        s = lax.dot_general(qbd, k_new, _NT, preferred_element_type=F32) - _rows_per_head(cn_ref[...], t)
        rr = lax.broadcasted_iota(jnp.int32, (rows, PAGE_SIZE), 0)
        cc = lax.broadcasted_iota(jnp.int32, (rows, PAGE_SIZE), 1)
        s = jnp.where(cc <= _imod(rr, t), s, NEG_INF)
        m_fin = jnp.maximum(m_new, jnp.max(s, axis=1, keepdims=True))
        pr = jnp.exp(s - m_fin)
        a_fin = jnp.exp(m_new - m_fin)
        l_fin = a_fin * l_new + jnp.sum(pr, axis=1, keepdims=True)
        o_wide = jnp.dot(pr.astype(BF16), v_new, preferred_element_type=F32)
        for h in range(n_fox):
            blk = slice(h * t, (h + 1) * t)
            lanes = slice(h * HEAD_DIM, (h + 1) * HEAD_DIM)
            o_h = a_fin[blk] * acc_new[blk] + o_wide[blk, lanes]
            o_ref[:, lanes] = (o_h / l_fin[blk]).astype(o_ref.dtype)


def _fox_sample(q, k_new, v_new, cn_pad, cache_k, cache_v, page_et, page_table, t, n_fox):
    db, n_pages = page_table.shape
    width = n_fox * HEAD_DIM
    keys = PAGE_SIZE * n_fox
    n_grp = _pick(n_pages, (16, 8, 4, 2, 1))
    last = n_pages - 1

    def kv_spec(g):
        return pl.BlockSpec((None, None, PAGE_SIZE, n_fox, HEAD_DIM),
                            lambda b, p, pt: (0, pt[b, last - p * n_grp - g], 0, 0, 0))

    def et_spec(g):
        return pl.BlockSpec((None, 2, keys), lambda b, p, pt: (pt[b, last - p * n_grp - g], 0, 0))

    rng = range(n_grp)
    grid_spec = pltpu.PrefetchScalarGridSpec(
        num_scalar_prefetch=1,
        grid=(db, n_pages // n_grp),
        in_specs=[
            pl.BlockSpec((t, width), lambda b, p, pt: (b, 0)),
            pl.BlockSpec((t, width), lambda b, p, pt: (b, 0)),
            pl.BlockSpec((t, width), lambda b, p, pt: (b, 0)),
            pl.BlockSpec((None, n_fox, PAGE_SIZE), lambda b, p, pt: (b, 0, 0)),
        ] + [kv_spec(g) for g in rng] + [kv_spec(g) for g in rng] + [et_spec(g) for g in rng],
        out_specs=pl.BlockSpec((t, width), lambda b, p, pt: (b, 0)),
        scratch_shapes=[
            pltpu.VMEM((n_fox * t, HEAD_DIM), BF16),
            pltpu.VMEM((n_fox * t, keys), F32),
            pltpu.VMEM((n_fox * t, 1), F32),
            pltpu.VMEM((n_fox * t, 1), F32),
            pltpu.VMEM((n_fox * t, HEAD_DIM), F32),
            pltpu.VMEM((1, keys), F32),
        ],
    )
    return pl.pallas_call(
        functools.partial(_fox_sample_body, scale=HEAD_DIM ** -0.5, n_fox=n_fox, n_grp=n_grp),
        grid_spec=grid_spec,
        out_shape=jax.ShapeDtypeStruct((db * t, width), F32),
        compiler_params=_params("parallel", "arbitrary"),
        name="fox_sample",
    )(page_table, q, k_new, v_new, cn_pad, *([cache_k] * n_grp), *([cache_v] * n_grp), *([page_et] * n_grp))


def _rotate(x, cos_f, sin_f):
    return x * cos_f + pltpu.roll(x, HEAD_DIM // 2, 1) * sin_f


def _retention_body(q_ref, k_ref, v_ref, g_ref, cos_ref, sin_ref, dec_ref, qd_ref, kd_ref, cd_ref, gn_ref,
                    s0_ref, o_ref, sout_ref, state, *, k_scale):
    c = pl.program_id(1)
    n_ret = state.shape[0]

    @pl.when(c == 0)
    def _():
        state[...] = s0_ref[...]

    cos_f = cos_ref[...]
    sin_f = sin_ref[...]
    for h in range(n_ret):
        lanes = slice(h * HEAD_DIM, (h + 1) * HEAD_DIM)
        q = _rotate(q_ref[:, lanes], cos_f, sin_f)
        k = _rotate(k_ref[:, lanes], cos_f, sin_f) * k_scale
        qb = q.astype(BF16)
        vb = v_ref[:, lanes].astype(BF16)
        s_in = state[h]
        scores = lax.dot_general(qb, k.astype(BF16), _NT, preferred_element_type=F32) * dec_ref[h]
        y = jnp.dot(scores.astype(BF16), vb, preferred_element_type=F32)
        y = y + jnp.dot(qb, s_in.astype(BF16), preferred_element_type=F32) * qd_ref[h]
        kd = (k * kd_ref[h]).astype(BF16)
        new_state = cd_ref[h] * s_in + lax.dot_general(kd, vb, _TN, preferred_element_type=F32)
        state[h] = new_state
        mu = jnp.mean(y, axis=1, keepdims=True)
        yc = y - mu
        var = jnp.mean(yc * yc, axis=1, keepdims=True)
        yn = yc * lax.rsqrt(var + GN_EPS) * gn_ref[:, lanes]
        g = g_ref[:, lanes]
        o_ref[:, lanes] = (g / (1.0 + jnp.exp(-g)) * yn).astype(o_ref.dtype)

    @pl.when(c == pl.num_programs(1) - 1)
    def _():
        sout_ref[...] = state[...]


def _retention(z, row0, col0, n_ret, batch, seq, chunk, state0, cos_f, sin_f, tables, gn_row):
    dec, qd, kd, cd = tables
    nc = seq // chunk
    rb0 = row0 // chunk
    d = HEAD_DIM
    width = n_ret * d
    assert col0 % width == 0
    cb0 = col0 // width
    zspec = lambda off: pl.BlockSpec((chunk, width), lambda b, c: (rb0 + b * nc + c, cb0 + off))
    full = lambda a: pl.BlockSpec(a.shape, lambda b, c: (0,) * a.ndim)
    out, s_fin = pl.pallas_call(
        functools.partial(_retention_body, k_scale=HEAD_DIM ** -0.5),
        grid=(batch, nc),
        in_specs=[
            zspec(0), zspec(1), zspec(2), zspec(3),
            pl.BlockSpec((chunk, d), lambda b, c: (c, 0)),
            pl.BlockSpec((chunk, d), lambda b, c: (c, 0)),
            full(dec), full(qd), full(kd), full(cd), full(gn_row),
            pl.BlockSpec((None, n_ret, d, d), lambda b, c: (b, 0, 0, 0)),
        ],
        out_specs=[
            pl.BlockSpec((chunk, width), lambda b, c: (b * nc + c, 0)),
            pl.BlockSpec((None, n_ret, d, d), lambda b, c: (b, 0, 0, 0)),
        ],
        out_shape=[
            jax.ShapeDtypeStruct((batch * seq, width), BF16 if chunk % 16 == 0 else F32),
            jax.ShapeDtypeStruct((batch, n_ret, d, d), F32),
        ],
        scratch_shapes=[pltpu.VMEM((n_ret, d, d), F32)],
        compiler_params=_params("parallel", "arbitrary"),
        name=f"retention_{chunk}",
    )(z, z, z, z, cos_f, sin_f, dec, qd, kd, cd, gn_row, state0)
    return out, s_fin


def _retention_tables(n_ret, chunk):
    log_gamma = jnp.log1p(-jnp.exp2(-5.0 - jnp.arange(n_ret, dtype=F32)))
    pos = jnp.arange(chunk, dtype=F32)
    diff = pos[:, None] - pos[None, :]
    dec = jnp.where((diff >= 0)[None], jnp.exp(jnp.maximum(diff, 0.0)[None] * log_gamma[:, None, None]), 0.0)
    qd = jnp.exp((pos[None, :] + 1.0) * log_gamma[:, None])
    kd = jnp.exp((chunk - 1.0 - pos)[None, :] * log_gamma[:, None])
    cd = jnp.exp(chunk * log_gamma)
    bc = lambda a: jnp.broadcast_to(a[:, :, None], (n_ret, chunk, HEAD_DIM))
    return dec, bc(qd), bc(kd), jnp.broadcast_to(cd[:, None, None], (n_ret, 1, HEAD_DIM))


def _rotary_tables(pos):
    half = HEAD_DIM // 2
    inv_freq = ROPE_BASE ** (-jnp.arange(half, dtype=F32) / half)
    ang = pos[:, None] * inv_freq[None, :]
    cos, sin = jnp.cos(ang), jnp.sin(ang)
    return jnp.concatenate([cos, cos], -1), jnp.concatenate([-sin, sin], -1)


def _layer_norm(x, g, b):
    mu = jnp.mean(x, axis=1, keepdims=True)
    xc = x - mu
    var = jnp.mean(xc * xc, axis=1, keepdims=True)
    return xc * lax.rsqrt(var + LN_EPS) * g + b


def _group_specs(tm, width, tiles_p):
    return (pl.BlockSpec((tm, width), lambda i: (jnp.minimum(i, tiles_p - 1), 0)),
            pl.BlockSpec((tm, width), lambda i: (jnp.maximum(i - tiles_p, 0), 0)))


def _wo_ln_body(xp_ref, xs_ref, ofp_ref, ofs_ref, orp_ref, ors_ref, w1_ref, w2_ref, g_ref, b_ref, h_ref, hb_ref,
                *, alpha, tiles_p):
    is_p = pl.program_id(0) < tiles_p
    x = jnp.where(is_p, xp_ref[...], xs_ref[...])
    o_fox = jnp.where(is_p, ofp_ref[...], ofs_ref[...].astype(BF16))
    o_ret = jnp.where(is_p, orp_ref[...], ors_ref[...].astype(BF16))
    mixed = jnp.dot(o_fox, w1_ref[...], preferred_element_type=F32)
    mixed = mixed + jnp.dot(o_ret, w2_ref[...], preferred_element_type=F32)
    h = _layer_norm(alpha * x + mixed, g_ref[...], b_ref[...])
    h_ref[...] = h
    hb_ref[...] = h.astype(BF16)


def _wo_ln(x_p, x_s, of_p, of_s, or_p, or_s, w1, w2, g, b, alpha, tm):
    d = x_p.shape[1]
    tiles_p, tiles_s = x_p.shape[0] // tm, x_s.shape[0] // tm
    n = x_p.shape[0] + x_s.shape[0]
    row = lambda w: pl.BlockSpec((tm, w), lambda i: (i, 0))
    full = lambda a: pl.BlockSpec(a.shape, lambda i: (0, 0))
    return pl.pallas_call(
        functools.partial(_wo_ln_body, alpha=alpha, tiles_p=tiles_p),
        grid=(tiles_p + tiles_s,),
        in_specs=[*_group_specs(tm, d, tiles_p), *_group_specs(tm, of_p.shape[1], tiles_p),
                  *_group_specs(tm, or_p.shape[1], tiles_p), full(w1), full(w2), full(g), full(b)],
        out_specs=[row(d), row(d)],
        out_shape=[jax.ShapeDtypeStruct((n, d), F32), jax.ShapeDtypeStruct((n, d), BF16)],
        compiler_params=_params("arbitrary"),
        name="wo_ln1",
    )(x_p, x_s, of_p, of_s, or_p, or_s, w1, w2, g, b)


def _top_desc(s, count, n_masked, one_per_round):
    n = s.shape[0]

    vals, work = [], s
    if one_per_round:
        iota = lax.broadcasted_iota(jnp.int32, s.shape, 0)
        for _ in range(count):
            mj = jnp.max(work, axis=0, keepdims=True)
            vals.append(mj)
            first = jnp.min(jnp.where(work == mj, iota, n), axis=0, keepdims=True)
            work = jnp.where(iota == first, NEG_INF, work)
        return vals, jnp.zeros((1, s.shape[1]), F32)
    for _ in range(count):
        mj = jnp.max(work, axis=0, keepdims=True)
        vals.append(mj)
        work = jnp.where(work == mj, NEG_INF, work)
    removed = jnp.sum(jnp.where(work == NEG_INF, 1.0, 0.0), axis=0, keepdims=True)
    return vals, jnp.abs(removed - float(count + n_masked))


_CAND = [(a, b) for a in range(PEER_TOPK + 1) for b in range(PEER_TOPK + 1) if (a + 1) * (b + 1) <= PEER_TOPK + 1]


def _peer_score_body(hb_ref, wq_ref, keys_ref, t1_ref, e1_ref, e2_ref, q_sc):
    tm = hb_ref.shape[0]
    q_sc[...] = jnp.dot(hb_ref[...], wq_ref[...], preferred_element_type=F32).astype(BF16)

    def head(h, one_per_round):
        s = []
        for c in range(2):
            col = (2 * h + c) * PEER_HALF
            s.append(lax.dot_general(keys_ref[h, c], q_sc[:, col:col + PEER_HALF], _NT,
                                     preferred_element_type=F32))
        v1, bad1 = _top_desc(s[0], PEER_TOPK + 1, 0, one_per_round)
        v2, bad2 = _top_desc(s[1], PEER_TOPK + 1, 0, one_per_round)
        rows = [v1[a] + v2[b] for a, b in _CAND]
        pad = (-len(rows)) % 8
        cand = jnp.concatenate(rows + [jnp.full((pad, tm), NEG_INF, F32)], axis=0)
        top, bad3 = _top_desc(cand, PEER_TOPK + 1, pad, one_per_round)
        z = jnp.ones((1, tm), F32)
        for j in range(1, PEER_TOPK):
            z = z + jnp.exp(top[j] - top[0])
        theta = 0.5 * (top[PEER_TOPK - 1] + top[PEER_TOPK])
        c1 = s[0] - v1[0]
        t1_ref[h] = jnp.exp((theta - top[0]) - c1)
        e1_ref[h] = jnp.exp(c1 - jnp.log(z))
        e2_ref[h] = jnp.exp(s[1] - v2[0])
        return jnp.max(bad1 + bad2 + bad3)

    bad = [head(h, False) for h in range(PEER_HEADS)]
    for h in range(PEER_HEADS):
        @pl.when(bad[h] > 0.0)
        def _():
            head(h, True)


def _peer_scores(hb, wq, keys, tm):
    n, d = hb.shape
    return pl.pallas_call(
        _peer_score_body,
        grid=(n // tm,),
        in_specs=[pl.BlockSpec((tm, d), lambda i: (i, 0)), pl.BlockSpec(wq.shape, lambda i: (0, 0)),
                  pl.BlockSpec(keys.shape, lambda i: (0, 0, 0, 0))],
        out_specs=[pl.BlockSpec((PEER_HEADS, PEER_KEYS, tm), lambda i: (0, 0, i))] * 3,
        out_shape=[jax.ShapeDtypeStruct((PEER_HEADS, PEER_KEYS, n), F32)] * 3,
        scratch_shapes=[pltpu.VMEM((tm, wq.shape[1]), BF16)],
        compiler_params=_params("parallel"),
        name="peer_scores",
    )(hb, wq, keys)


PEER_SUB = 256


PEER_EB = 2 * PEER_SUB


def _peer_dense_body(hb_ref, u_ref, v0_ref, vp_ref, t1_ref, e1_ref, e2_ref, o_ref, act_sc, w_sc, acc_sc):
    j = pl.program_id(1)
    n_steps = pl.num_programs(1) - 1
    tm = hb_ref.shape[0]
    n_i1 = PEER_SUB // PEER_KEYS

    @pl.when(j == 0)
    def _():
        o_ref[...] = jnp.zeros(o_ref.shape, F32)
        acc_sc[...] = jnp.zeros(acc_sc.shape, F32)
        w_sc[1] = jnp.zeros(w_sc.shape[1:], BF16)

    d_model = o_ref.shape[1]
    n_tiles = tm // 128
    col_chunks = [slice(q * 256, (q + 1) * 256) for q in range(d_model // 256)]

    def product(slot, v_ref, cols):
        dst = acc_sc if slot == 1 else o_ref
        dst[:, cols] += jnp.dot(w_sc[slot], v_ref[:, cols].astype(BF16), preferred_element_type=F32)

    def build_w(r, prev_slot, prev_v):
        act = lax.dot_general(hb_ref[...], u_ref[r * PEER_SUB:(r + 1) * PEER_SUB, :].astype(BF16), _NT,
                              preferred_element_type=F32)
        act_sc[r] = 0.5 * act * (1.0 + lax.erf(act * (2.0 ** -0.5)))
        i1_0 = (2 * j + r) * n_i1
        t1_rows = [[t1_ref[h, pl.ds(i1_0 + k, 1), :] for k in range(n_i1)] for h in range(PEER_HEADS)]
        e1_rows = [[e1_ref[h, pl.ds(i1_0 + k, 1), :] for k in range(n_i1)] for h in range(PEER_HEADS)]
        pending_chunks = list(col_chunks)
        for c in range(n_tiles):
            lanes = slice(c * 128, (c + 1) * 128)
            for k in range(n_i1):
                gate = None
                for h in range(PEER_HEADS):
                    e2 = e2_ref[h, :, lanes]
                    g = jnp.where(e2 >= t1_rows[h][k][:, lanes], e2, 0.0) * e1_rows[h][k][:, lanes]
                    gate = g if gate is None else gate + g
                cols = slice(k * PEER_KEYS, (k + 1) * PEER_KEYS)
                w_sc[r, lanes, cols] = (act_sc[r, lanes, cols] * gate.T).astype(BF16)
            n_now = -(-len(pending_chunks) // (n_tiles - c))
            for _ in range(n_now):
                product(prev_slot, prev_v, pending_chunks.pop(0))

    @pl.when(j < n_steps)
    def _main():
        build_w(0, 1, vp_ref)
        build_w(1, 0, v0_ref)

    @pl.when(j == n_steps)
    def _drain():
        for cols in col_chunks:
            product(1, vp_ref, cols)
        o_ref[...] += acc_sc[...]


def _peer_dense(hb, u, v, t1, e1, e2, tm):
    n, d = hb.shape
    n_steps = u.shape[0] // PEER_EB
    last = n_steps - 1
    per_tile = lambda shape, imap: pl.BlockSpec(shape, imap, pipeline_mode=pl.Buffered(1))
    gate_spec = lambda: per_tile((PEER_HEADS, PEER_KEYS, tm), lambda i, j: (0, 0, i))
    return pl.pallas_call(
        _peer_dense_body,
        grid=(n // tm, n_steps + 1),
        in_specs=[
            per_tile((tm, d), lambda i, j: (i, 0)),
            pl.BlockSpec((PEER_EB, d), lambda i, j: (jnp.minimum(j, last), 0)),
            pl.BlockSpec((PEER_SUB, d), lambda i, j: (2 * jnp.minimum(j, last), 0)),
            pl.BlockSpec((PEER_SUB, d), lambda i, j: (jnp.maximum(2 * j - 1, 0), 0)),
            gate_spec(), gate_spec(), gate_spec(),
        ],
        out_specs=pl.BlockSpec((tm, d), lambda i, j: (i, 0)),
        out_shape=jax.ShapeDtypeStruct((n, d), F32),
        scratch_shapes=[pltpu.VMEM((2, tm, PEER_SUB), F32), pltpu.VMEM((2, tm, PEER_SUB), BF16),
                        pltpu.VMEM((tm, d), F32)],
        compiler_params=_params("parallel", "arbitrary"),
        name="peer_dense",
    )(hb, u, v, v, t1, e1, e2)


def _out_body(h_ref, po_ref, pp_ref, ps_ref, g_ref, b_ref, wg_ref, bg_ref, we_ref, yp_ref, ys_ref,
              *, alpha, tiles_p):
    i = pl.program_id(0)
    h2 = _layer_norm(alpha * h_ref[...] + po_ref[...], g_ref[...], b_ref[...])
    lin = jnp.dot(h2.astype(BF16), wg_ref[...], preferred_element_type=F32) + bg_ref[...]
    p_emb = jnp.where(i < tiles_p, pp_ref[...], ps_ref[...])
    emb = jnp.dot(p_emb.astype(BF16), we_ref[...], preferred_element_type=F32)
    y = h2 + emb / (1.0 + jnp.exp(-lin))

    @pl.when(i < tiles_p)
    def _():
        yp_ref[...] = y

    @pl.when(i >= tiles_p)
    def _():
        ys_ref[...] = y


def _out_stage(h1, peer_out, p_p, p_s, g, b, wg, bg, we, alpha, tm):
    n, d = h1.shape
    tiles_p, tiles_s = p_p.shape[0] // tm, p_s.shape[0] // tm
    assert (tiles_p + tiles_s) * tm == n
    row = lambda w: pl.BlockSpec((tm, w), lambda i: (i, 0))
    full = lambda a: pl.BlockSpec(a.shape, lambda i: (0, 0))
    return pl.pallas_call(
        functools.partial(_out_body, alpha=alpha, tiles_p=tiles_p),
        grid=(tiles_p + tiles_s,),
        in_specs=[row(d), row(d), *_group_specs(tm, p_p.shape[1], tiles_p),
                  full(g), full(b), full(wg), full(bg), full(we)],
        out_specs=list(_group_specs(tm, d, tiles_p)),
        out_shape=[jax.ShapeDtypeStruct((tiles_p * tm, d), F32), jax.ShapeDtypeStruct((tiles_s * tm, d), F32)],
        compiler_params=_params("arbitrary"),
        name="ln2_gate_out",
    )(h1, peer_out, p_p, p_s, g, b, wg, bg, we)


def kernel(x_prompt, x_sample, cache_k, cache_v, cache_logf, state_ret, page_table, p_prompt, p_sample, w_in, b_f,
           gn_g, w_o, ln1_g, ln1_b, w_pq, peer_keys, peer_u, peer_v, ln2_g, ln2_b, w_pg, b_pg, w_pe):
    depth = w_in.shape[0]
    assert depth == 1, "one layer"
    batch, seq, d_model = x_prompt.shape
    db, t_dec, _ = x_sample.shape
    n_pool = cache_k.shape[1]
    n_fox = cache_k.shape[3]
    n_heads = d_model // HEAD_DIM
    n_ret = n_heads - n_fox
    w_fox = n_fox * HEAD_DIM
    w_ret = n_ret * HEAD_DIM
    n_pages = page_table.shape[1]
    past_len = n_pages * PAGE_SIZE
    np_tok = batch * seq
    ns_tok = db * t_dec
    n_tok = np_tok + ns_tok
    alpha = (2.0 * depth) ** 0.25
    assert cache_k.shape[2] == PAGE_SIZE and cache_k.shape[4] == HEAD_DIM and t_dec % 8 == 0

    tm_big = _pick(n_tok, (768, 512, 384, 256, 128))
    tm_mid = _pick(math.gcd(np_tok, ns_tok), (256, 128))

    wi = w_in[0]
    f0 = 3 * w_fox
    w_main = jnp.concatenate([wi[:, :f0], wi[:, f0 + n_fox:]], axis=1).astype(BF16)
    wf_t = wi[:, f0:f0 + n_fox].T.astype(BF16)
    wo1 = w_o[0, :w_fox].astype(BF16)
    wo2 = w_o[0, w_fox:].astype(BF16)
    wq = w_pq[0].astype(BF16)
    keys = peer_keys[0].astype(BF16)
    u_b = peer_u[0]
    v_b = peer_v[0]
    wg = w_pg[0].astype(BF16)
    we = w_pe[0].astype(BF16)
    row = lambda a: a.reshape(1, -1)

    x_p = x_prompt.reshape(np_tok, d_model)
    x_s = x_sample.reshape(ns_tok, d_model)
    b_col = b_f[0].reshape(n_fox, 1)

    qf_p, kf_p, vf_p, zr_p, lf_p = _in_proj(x_p, w_main, wf_t, b_col, w_fox)
    qf_s, kf_s, vf_s, zr_s, lf_s = _in_proj(x_s, w_main, wf_t, b_col, w_fox)

    c_t = _segment_cumsum(lf_p, seq)
    o_fox_p = _fox_prompt(qf_p, kf_p, vf_p, c_t, batch, seq, n_fox)

    cn_t = _segment_cumsum(jnp.pad(lf_s, ((0, 0), (0, (-ns_tok) % 128))), t_dec)[:, :ns_tok]
    cn = cn_t.reshape(n_fox, db, t_dec).transpose(1, 0, 2)
    cn_pad = jnp.pad(cn, ((0, 0), (0, 0), (0, PAGE_SIZE - t_dec)), constant_values=BIG)
    page_et = _page_suffix(cache_logf[0].reshape(n_pool, PAGE_SIZE * n_fox))
    o_fox_s = _fox_sample(qf_s, kf_s, vf_s, cn_pad, cache_k, cache_v, page_et, page_table, t_dec, n_fox)

    gn_row = row(gn_g[0])
    cos_p, sin_p = _rotary_tables(jnp.arange(seq, dtype=F32))
    cos_s, sin_s = _rotary_tables(past_len + jnp.arange(t_dec, dtype=F32))
    zeros_state = jnp.zeros((batch, n_ret, HEAD_DIM, HEAD_DIM), F32)
    o_ret_p, s_p = _retention(zr_p, 0, 0, n_ret, batch, seq, RET_CHUNK, zeros_state, cos_p, sin_p,
                              _retention_tables(n_ret, RET_CHUNK), gn_row)
    o_ret_s, s_s = _retention(zr_s, 0, 0, n_ret, db, t_dec, t_dec, state_ret[0], cos_s, sin_s,
                              _retention_tables(n_ret, t_dec), gn_row)

    h1, h1b = _wo_ln(x_p, x_s, o_fox_p, o_fox_s, o_ret_p, o_ret_s, wo1, wo2, row(ln1_g[0]), row(ln1_b[0]),
                     alpha, tm_mid)
    t1, e1, e2 = _peer_scores(h1b, wq, keys, _pick(n_tok, (128,)))
    peer_out = _peer_dense(h1b, u_b, v_b, t1, e1, e2, tm_big)
    y_p, y_s = _out_stage(h1, peer_out, p_prompt[0].reshape(np_tok, -1), p_sample[0].reshape(ns_tok, -1),
                          row(ln2_g[0]), row(ln2_b[0]), wg, row(b_pg[0]), we, alpha, tm_mid)

    heads_p = lambda a: a.reshape(1, batch, seq, n_fox, HEAD_DIM)
    heads_s = lambda a: a.reshape(1, db, t_dec, n_fox, HEAD_DIM)
    return (y_p.reshape(batch, seq, d_model), y_s.reshape(db, t_dec, d_model),
            heads_p(kf_p), heads_p(vf_p), lf_p.T.reshape(1, batch, seq, n_fox), s_p[None],
            heads_s(kf_s), heads_s(vf_s), lf_s.T.reshape(1, db, t_dec, n_fox), s_s[None])
```

```python
import functools
import math

import numpy as np
import jax
import jax.numpy as jnp
from jax import lax
from jax.experimental import pallas as pl
from jax.experimental.pallas import tpu as pltpu

F32 = jnp.float32
BF16 = jnp.bfloat16

HEAD_DIM = 128
PAGE_SIZE = 128
ROPE_BASE = 10000.0
RET_CHUNK = 128
PEER_HEADS = 8
PEER_KEYS = 128
PEER_HALF = 128
PEER_TOPK = 16
LN_EPS = 1e-5
GN_EPS = 1e-6
NEG_INF = float("-inf")
BIG = 1e30

VMEM_LIMIT = 56 * 1024 * 1024

_NT = (((1,), (1,)), ((), ()))
_TN = (((0,), (0,)), ((), ()))


def _pick(n, candidates):
    for c in candidates:
        if n % c == 0:
            return c
    raise ValueError(f"no tile in {candidates} divides {n}")


def _idiv(x, c):
    assert c & (c - 1) == 0
    return lax.shift_right_logical(x, int(math.log2(c)))


def _imod(x, c):
    assert c & (c - 1) == 0
    return x & (c - 1)


def _params(*sem):
    return pltpu.CompilerParams(dimension_semantics=sem, vmem_limit_bytes=VMEM_LIMIT)


def _in_proj_body(x_ref, w_ref, wf_ref, b_ref, q_ref, k_ref, v_ref, zr_ref, lf_ref, xb_sc):
    j = pl.program_id(1)

    @pl.when(j == 0)
    def _():
        xb_sc[...] = x_ref[...].astype(BF16)

    xb = xb_sc[...]
    r = jnp.dot(xb, w_ref[...], preferred_element_type=F32)

    @pl.when(j == 0)
    def _():
        q_ref[...] = r
        f = lax.dot_general(wf_ref[...], xb, (((0,), (1,)), ((), ())),
                            preferred_element_type=F32) + b_ref[...]
        lf_ref[...] = jnp.minimum(f, 0.0) - jnp.log1p(jnp.exp(-jnp.abs(f)))

    @pl.when(j == 1)
    def _():
        k_ref[...] = r

    @pl.when(j == 2)
    def _():
        v_ref[...] = r

    @pl.when(j >= 3)
    def _():
        zr_ref[...] = r


def _in_proj(x, w_main, wf, b_col, w_fox):
    m, d = x.shape
    n_cols = w_main.shape[1]
    n_h = wf.shape[1]
    tm = _pick(m, (512, 256, 128))
    nj = n_cols // w_fox
    assert nj * w_fox == n_cols and nj > 3
    head = lambda: pl.BlockSpec((tm, w_fox), lambda i, j: (i, 0))
    return pl.pallas_call(
        _in_proj_body,
        grid=(m // tm, nj),
        in_specs=[pl.BlockSpec((tm, d), lambda i, j: (i, 0)), pl.BlockSpec((d, w_fox), lambda i, j: (0, j)),
                  pl.BlockSpec((d, n_h), lambda i, j: (0, 0)), pl.BlockSpec((n_h, 1), lambda i, j: (0, 0))],
        out_specs=[head(), head(), head(),
                   pl.BlockSpec((tm, w_fox), lambda i, j: (i, jnp.maximum(j - 3, 0))),
                   pl.BlockSpec((n_h, tm), lambda i, j: (0, i))],
        out_shape=[jax.ShapeDtypeStruct((m, w_fox), F32)] * 3
        + [jax.ShapeDtypeStruct((m, n_cols - 3 * w_fox), F32), jax.ShapeDtypeStruct((n_h, m), F32)],
        scratch_shapes=[pltpu.VMEM((tm, d), BF16)],
        compiler_params=_params("parallel", "arbitrary"),
        name="in_proj",
    )(x, w_main, wf, b_col)


def _split3(x):
    hi = x.astype(BF16)
    r1 = x - hi.astype(F32)
    mid = r1.astype(BF16)
    lo = (r1 - mid.astype(F32)).astype(BF16)
    return hi, mid, lo


def _dot3(parts, m):
    acc = jnp.dot(parts[0], m, preferred_element_type=F32)
    acc = acc + jnp.dot(parts[1], m, preferred_element_type=F32)
    return acc + jnp.dot(parts[2], m, preferred_element_type=F32)


def _cumsum_body(x_ref, o_ref, *, seg, blk):
    rows, length = x_ref.shape
    r = lax.broadcasted_iota(jnp.int32, (blk, blk), 0)
    c = lax.broadcasted_iota(jnp.int32, (blk, blk), 1)
    if seg >= blk:
        tri = (r <= c)
    else:
        tri = (r <= c) & (_idiv(r, seg) == _idiv(c, seg))
    tri = jnp.where(tri, 1.0, 0.0).astype(BF16)
    carry = jnp.zeros((rows, 1), F32)
    for j in range(length // blk):
        if seg < blk or (j * blk) % seg == 0:
            carry = jnp.zeros((rows, 1), F32)
        cs = _dot3(_split3(x_ref[:, j * blk:(j + 1) * blk]), tri) + carry
        o_ref[:, j * blk:(j + 1) * blk] = cs
        carry = cs[:, blk - 1:blk]


def _segment_cumsum(x, seg):
    rows, length = x.shape
    blk = _pick(length, (256, 128))
    assert seg % blk == 0 or blk % seg == 0
    return pl.pallas_call(
        functools.partial(_cumsum_body, seg=seg, blk=blk),
        out_shape=jax.ShapeDtypeStruct((rows, length), F32),
        compiler_params=pltpu.CompilerParams(vmem_limit_bytes=VMEM_LIMIT),
        name="segment_cumsum",
    )(x)


def _fox_prompt_body(q_ref, k_ref, v_ref, c_ref, o_ref, *, tq, scale):
    s_len = q_ref.shape[0]
    n_h = q_ref.shape[1] // HEAD_DIM
    row = lax.broadcasted_iota(jnp.int32, (tq, tq), 0)
    col = lax.broadcasted_iota(jnp.int32, (tq, tq), 1)
    causal = col <= row

    def step(qs, kj, carry, mask):
        ks = kj * tq if isinstance(kj, int) else pl.multiple_of(kj * tq, tq)
        out = []
        for h in range(n_h):
            m, l, acc = carry[h]
            lanes = slice(h * HEAD_DIM, (h + 1) * HEAD_DIM)
            k = k_ref[pl.ds(ks, tq), lanes].astype(BF16)
            v = v_ref[pl.ds(ks, tq), lanes].astype(BF16)
            s = lax.dot_general(qs[h], k, _NT, preferred_element_type=F32) - c_ref[h, kj]
            if mask:
                s = jnp.where(causal, s, NEG_INF)
            m_new = jnp.maximum(m, jnp.max(s, axis=1, keepdims=True))
            p = jnp.exp(s - m_new)
            alpha = jnp.exp(m - m_new)
            l = alpha * l + jnp.sum(p, axis=1, keepdims=True)
            acc = alpha * acc + jnp.dot(p.astype(BF16), v, preferred_element_type=F32)
            out.append((m_new, l, acc))
        return tuple(out)

    for qi in range(s_len // tq):
        qs = [(q_ref[qi * tq:(qi + 1) * tq, h * HEAD_DIM:(h + 1) * HEAD_DIM] * scale).astype(BF16)
              for h in range(n_h)]
        carry = tuple((jnp.full((tq, 1), NEG_INF, F32), jnp.zeros((tq, 1), F32), jnp.zeros((tq, HEAD_DIM), F32))
                      for _ in range(n_h))
        if qi > 0:
            carry = lax.fori_loop(0, qi, lambda kj, cr: step(qs, kj, cr, False), carry)
        carry = step(qs, qi, carry, True)
        for h in range(n_h):
            m, l, acc = carry[h]
            o_ref[qi * tq:(qi + 1) * tq, h * HEAD_DIM:(h + 1) * HEAD_DIM] = (acc / l).astype(o_ref.dtype)


FOX_HEADS_PER_STEP = 4


def _fox_prompt(q, k, v, c_t, batch, seq, n_fox):
    tq = _pick(seq, (256, 128))
    hp = FOX_HEADS_PER_STEP
    assert n_fox % hp == 0
    c4 = c_t.reshape(n_fox, batch * seq // tq, 1, tq)
    nb = seq // tq
    groups = n_fox // hp
    width = hp * HEAD_DIM
    return pl.pallas_call(
        functools.partial(_fox_prompt_body, tq=tq, scale=HEAD_DIM ** -0.5),
        grid=(batch, groups),
        in_specs=[
            pl.BlockSpec((seq, width), lambda b, h: (b, h)),
            pl.BlockSpec((seq, width), lambda b, h: (b, h)),
            pl.BlockSpec((seq, width), lambda b, h: (b, h)),
            pl.BlockSpec((hp, nb, 1, tq), lambda b, h: (h, b, 0, 0)),
        ],
        out_specs=pl.BlockSpec((seq, width), lambda b, h: (b, h)),
        out_shape=jax.ShapeDtypeStruct((batch * seq, n_fox * HEAD_DIM), BF16),
        compiler_params=_params("parallel", "parallel"),
        name="fox_prompt",
    )(q, k, v, c4)


def _page_suffix_body(lp_ref, et_ref):
    n = lp_ref.shape[1]
    n_h = n // PAGE_SIZE
    r = lax.broadcasted_iota(jnp.int32, (n, n), 0)
    c = lax.broadcasted_iota(jnp.int32, (n, n), 1)
    same_head = _imod(r, n_h) == _imod(c, n_h)
    later = _idiv(r, n_h) > _idiv(c, n_h)
    m_e = jnp.where(same_head & later, 1.0, 0.0).astype(BF16)
    m_t = jnp.where(same_head, 1.0, 0.0).astype(BF16)
    parts = _split3(lp_ref[...])
    et_ref[:, 0, :] = _dot3(parts, m_e)
    et_ref[:, 1, :] = _dot3(parts, m_t)


def _page_suffix(lp_rows):
    n_pool, n = lp_rows.shape
    tr = _pick(n_pool, (512, 256, 128, 64, 32, 16, 8))
    return pl.pallas_call(
        _page_suffix_body,
        grid=(n_pool // tr,),
        in_specs=[pl.BlockSpec((tr, n), lambda i: (i, 0))],
        out_specs=pl.BlockSpec((tr, 2, n), lambda i: (i, 0, 0)),
        out_shape=jax.ShapeDtypeStruct((n_pool, 2, n), F32),
        compiler_params=_params("parallel"),
        name="page_suffix",
    )(lp_rows)


def _rows_per_head(x, t):
    h, l = x.shape
    return jnp.broadcast_to(x[:, None, :], (h, t, l)).reshape(h * t, l)


def _fox_sample_body(pt_ref, q_ref, kn_ref, vn_ref, cn_ref, *refs, scale, n_fox, n_grp):
    k_refs = refs[:n_grp]
    v_refs = refs[n_grp:2 * n_grp]
    et_refs = refs[2 * n_grp:3 * n_grp]
    o_ref, q_sc, hm_sc, m_sc, l_sc, acc, carry = refs[3 * n_grp:]
    p = pl.program_id(1)
    t = q_ref.shape[0]
    rows = n_fox * t
    keys = PAGE_SIZE * n_fox

    @pl.when(p == 0)
    def _init():
        q_sc[...] = (jnp.concatenate([q_ref[:, h * HEAD_DIM:(h + 1) * HEAD_DIM] for h in range(n_fox)], axis=0)
                     * scale).astype(BF16)
        r = lax.broadcasted_iota(jnp.int32, (rows, keys), 0)
        c = lax.broadcasted_iota(jnp.int32, (rows, keys), 1)
        hm_sc[...] = jnp.where(_idiv(r, t) == _imod(c, n_fox), 0.0, NEG_INF)
        m_sc[...] = jnp.full(m_sc.shape, NEG_INF, F32)
        l_sc[...] = jnp.zeros(l_sc.shape, F32)
        acc[...] = jnp.zeros(acc.shape, F32)
        carry[...] = jnp.zeros(carry.shape, F32)

    q = q_sc[...]
    hm = hm_sc[...]
    c = carry[...]
    scores = []
    for g in range(n_grp):
        et = et_refs[g][...]
        kf = k_refs[g][...].reshape(keys, HEAD_DIM).astype(BF16)
        scores.append(lax.dot_general(q, kf, _NT, preferred_element_type=F32) + (et[0:1] + c) + hm)
        c = c + et[1:2]
    carry[...] = c
    m_old = m_sc[...]
    m_new = m_old
    for s in scores:
        m_new = jnp.maximum(m_new, jnp.max(s, axis=1, keepdims=True))
    alpha = jnp.exp(m_old - m_new)
    l_new = alpha * l_sc[...]
    acc_new = alpha * acc[...]
    for g in range(n_grp):
        pr = jnp.exp(scores[g] - m_new)
        l_new = l_new + jnp.sum(pr, axis=1, keepdims=True)
        vf = v_refs[g][...].reshape(keys, HEAD_DIM).astype(BF16)
        acc_new = acc_new + jnp.dot(pr.astype(BF16), vf, preferred_element_type=F32)
    m_sc[...] = m_new
    l_sc[...] = l_new
    acc[...] = acc_new

    @pl.when(p == pl.num_programs(1) - 1)
    def _fin():
        width = n_fox * HEAD_DIM
        r = lax.broadcasted_iota(jnp.int32, (rows, width), 0)
        cw = lax.broadcasted_iota(jnp.int32, (rows, width), 1)
        q_rep = jnp.concatenate([q_ref[...]] * n_fox, axis=0) * scale
        qbd = jnp.where(_idiv(r, t) == _idiv(cw, HEAD_DIM), q_rep, 0.0).astype(BF16)
        pad = jnp.zeros((PAGE_SIZE - t, width), BF16)
        k_new = jnp.concatenate([kn_ref[...].astype(BF16), pad], axis=0)
        v_new = jnp.concatenate([vn_ref[...].astype(BF16), pad], axis=0)
        s = lax.dot_general(qbd, k_new, _NT, preferred_element_type=F32) - _rows_per_head(cn_ref[...], t)
        rr = lax.broadcasted_iota(jnp.int32, (rows, PAGE_SIZE), 0)
        cc = lax.broadcasted_iota(jnp.int32, (rows, PAGE_SIZE), 1)
        s = jnp.where(cc <= _imod(rr, t), s, NEG_INF)
        m_fin = jnp.maximum(m_new, jnp.max(s, axis=1, keepdims=True))
        pr = jnp.exp(s - m_fin)
        a_fin = jnp.exp(m_new - m_fin)
        l_fin = a_fin * l_new + jnp.sum(pr, axis=1, keepdims=True)
        o_wide = jnp.dot(pr.astype(BF16), v_new, preferred_element_type=F32)
        for h in range(n_fox):
            blk = slice(h * t, (h + 1) * t)
            lanes = slice(h * HEAD_DIM, (h + 1) * HEAD_DIM)
            o_h = a_fin[blk] * acc_new[blk] + o_wide[blk, lanes]
            o_ref[:, lanes] = (o_h / l_fin[blk]).astype(o_ref.dtype)


def _fox_sample(q, k_new, v_new, cn_pad, cache_k, cache_v, page_et, page_table, t, n_fox):
    db, n_pages = page_table.shape
    width = n_fox * HEAD_DIM
    keys = PAGE_SIZE * n_fox
    n_grp = _pick(n_pages, (16, 8, 4, 2, 1))
    last = n_pages - 1

    def kv_spec(g):
        return pl.BlockSpec((None, None, PAGE_SIZE, n_fox, HEAD_DIM),
                            lambda b, p, pt: (0, pt[b, last - p * n_grp - g], 0, 0, 0))

    def et_spec(g):
        return pl.BlockSpec((None, 2, keys), lambda b, p, pt: (pt[b, last - p * n_grp - g], 0, 0))

    rng = range(n_grp)
    grid_spec = pltpu.PrefetchScalarGridSpec(
        num_scalar_prefetch=1,
        grid=(db, n_pages // n_grp),
        in_specs=[
            pl.BlockSpec((t, width), lambda b, p, pt: (b, 0)),
            pl.BlockSpec((t, width), lambda b, p, pt: (b, 0)),
            pl.BlockSpec((t, width), lambda b, p, pt: (b, 0)),
            pl.BlockSpec((None, n_fox, PAGE_SIZE), lambda b, p, pt: (b, 0, 0)),
        ] + [kv_spec(g) for g in rng] + [kv_spec(g) for g in rng] + [et_spec(g) for g in rng],
        out_specs=pl.BlockSpec((t, width), lambda b, p, pt: (b, 0)),
        scratch_shapes=[
            pltpu.VMEM((n_fox * t, HEAD_DIM), BF16),
            pltpu.VMEM((n_fox * t, keys), F32),
            pltpu.VMEM((n_fox * t, 1), F32),
            pltpu.VMEM((n_fox * t, 1), F32),
            pltpu.VMEM((n_fox * t, HEAD_DIM), F32),
            pltpu.VMEM((1, keys), F32),
        ],
    )
    return pl.pallas_call(
        functools.partial(_fox_sample_body, scale=HEAD_DIM ** -0.5, n_fox=n_fox, n_grp=n_grp),
        grid_spec=grid_spec,
        out_shape=jax.ShapeDtypeStruct((db * t, width), F32),
        compiler_params=_params("parallel", "arbitrary"),
        name="fox_sample",
    )(page_table, q, k_new, v_new, cn_pad, *([cache_k] * n_grp), *([cache_v] * n_grp), *([page_et] * n_grp))


def _rotate(x, cos_f, sin_f):
    return x * cos_f + pltpu.roll(x, HEAD_DIM // 2, 1) * sin_f


def _retention_body(q_ref, k_ref, v_ref, g_ref, cos_ref, sin_ref, dec_ref, qd_ref, kd_ref, cd_ref, gn_ref,
                    s0_ref, o_ref, sout_ref, state, *, k_scale):
    c = pl.program_id(1)
    n_ret = state.shape[0]

    @pl.when(c == 0)
    def _():
        state[...] = s0_ref[...]

    cos_f = cos_ref[...]
    sin_f = sin_ref[...]
    for h in range(n_ret):
        lanes = slice(h * HEAD_DIM, (h + 1) * HEAD_DIM)
        q = _rotate(q_ref[:, lanes], cos_f, sin_f)
        k = _rotate(k_ref[:, lanes], cos_f, sin_f) * k_scale
        qb = q.astype(BF16)
        vb = v_ref[:, lanes].astype(BF16)
        s_in = state[h]
        scores = lax.dot_general(qb, k.astype(BF16), _NT, preferred_element_type=F32) * dec_ref[h]
        y = jnp.dot(scores.astype(BF16), vb, preferred_element_type=F32)
        y = y + jnp.dot(qb, s_in.astype(BF16), preferred_element_type=F32) * qd_ref[h]
        kd = (k * kd_ref[h]).astype(BF16)
        new_state = cd_ref[h] * s_in + lax.dot_general(kd, vb, _TN, preferred_element_type=F32)
        state[h] = new_state
        mu = jnp.mean(y, axis=1, keepdims=True)
        yc = y - mu
        var = jnp.mean(yc * yc, axis=1, keepdims=True)
        yn = yc * lax.rsqrt(var + GN_EPS) * gn_ref[:, lanes]
        g = g_ref[:, lanes]
        o_ref[:, lanes] = (g / (1.0 + jnp.exp(-g)) * yn).astype(o_ref.dtype)

    @pl.when(c == pl.num_programs(1) - 1)
    def _():
        sout_ref[...] = state[...]


def _retention(z, row0, col0, n_ret, batch, seq, chunk, state0, cos_f, sin_f, tables, gn_row):
    dec, qd, kd, cd = tables
    nc = seq // chunk
    rb0 = row0 // chunk
    d = HEAD_DIM
    width = n_ret * d
    assert col0 % width == 0
    cb0 = col0 // width
    zspec = lambda off: pl.BlockSpec((chunk, width), lambda b, c: (rb0 + b * nc + c, cb0 + off))
    full = lambda a: pl.BlockSpec(a.shape, lambda b, c: (0,) * a.ndim)
    out, s_fin = pl.pallas_call(
        functools.partial(_retention_body, k_scale=HEAD_DIM ** -0.5),
        grid=(batch, nc),
        in_specs=[
            zspec(0), zspec(1), zspec(2), zspec(3),
            pl.BlockSpec((chunk, d), lambda b, c: (c, 0)),
            pl.BlockSpec((chunk, d), lambda b, c: (c, 0)),
            full(dec), full(qd), full(kd), full(cd), full(gn_row),
            pl.BlockSpec((None, n_ret, d, d), lambda b, c: (b, 0, 0, 0)),
        ],
        out_specs=[
            pl.BlockSpec((chunk, width), lambda b, c: (b * nc + c, 0)),
            pl.BlockSpec((None, n_ret, d, d), lambda b, c: (b, 0, 0, 0)),
        ],
        out_shape=[
            jax.ShapeDtypeStruct((batch * seq, width), BF16 if chunk % 16 == 0 else F32),
            jax.ShapeDtypeStruct((batch, n_ret, d, d), F32),
        ],
        scratch_shapes=[pltpu.VMEM((n_ret, d, d), F32)],
        compiler_params=_params("parallel", "arbitrary"),
        name=f"retention_{chunk}",
    )(z, z, z, z, cos_f, sin_f, dec, qd, kd, cd, gn_row, state0)
    return out, s_fin


def _retention_tables(n_ret, chunk):
    log_gamma = jnp.log1p(-jnp.exp2(-5.0 - jnp.arange(n_ret, dtype=F32)))
    pos = jnp.arange(chunk, dtype=F32)
    diff = pos[:, None] - pos[None, :]
    dec = jnp.where((diff >= 0)[None], jnp.exp(jnp.maximum(diff, 0.0)[None] * log_gamma[:, None, None]), 0.0)
    qd = jnp.exp((pos[None, :] + 1.0) * log_gamma[:, None])
    kd = jnp.exp((chunk - 1.0 - pos)[None, :] * log_gamma[:, None])
    cd = jnp.exp(chunk * log_gamma)
    bc = lambda a: jnp.broadcast_to(a[:, :, None], (n_ret, chunk, HEAD_DIM))
    return dec, bc(qd), bc(kd), jnp.broadcast_to(cd[:, None, None], (n_ret, 1, HEAD_DIM))


def _rotary_tables(pos):
    half = HEAD_DIM // 2
    inv_freq = ROPE_BASE ** (-jnp.arange(half, dtype=F32) / half)
    ang = pos[:, None] * inv_freq[None, :]
    cos, sin = jnp.cos(ang), jnp.sin(ang)
    return jnp.concatenate([cos, cos], -1), jnp.concatenate([-sin, sin], -1)


def _layer_norm(x, g, b):
    mu = jnp.mean(x, axis=1, keepdims=True)
    xc = x - mu
    var = jnp.mean(xc * xc, axis=1, keepdims=True)
    return xc * lax.rsqrt(var + LN_EPS) * g + b


def _group_specs(tm, width, tiles_p):
    return (pl.BlockSpec((tm, width), lambda i: (jnp.minimum(i, tiles_p - 1), 0)),
            pl.BlockSpec((tm, width), lambda i: (jnp.maximum(i - tiles_p, 0), 0)))


def _wo_ln_body(xp_ref, xs_ref, ofp_ref, ofs_ref, orp_ref, ors_ref, w1_ref, w2_ref, g_ref, b_ref, h_ref, hb_ref,
                *, alpha, tiles_p):
    is_p = pl.program_id(0) < tiles_p
    x = jnp.where(is_p, xp_ref[...], xs_ref[...])
    o_fox = jnp.where(is_p, ofp_ref[...], ofs_ref[...].astype(BF16))
    o_ret = jnp.where(is_p, orp_ref[...], ors_ref[...].astype(BF16))
    mixed = jnp.dot(o_fox, w1_ref[...], preferred_element_type=F32)
    mixed = mixed + jnp.dot(o_ret, w2_ref[...], preferred_element_type=F32)
    h = _layer_norm(alpha * x + mixed, g_ref[...], b_ref[...])
    h_ref[...] = h
    hb_ref[...] = h.astype(BF16)


def _wo_ln(x_p, x_s, of_p, of_s, or_p, or_s, w1, w2, g, b, alpha, tm):
    d = x_p.shape[1]
    tiles_p, tiles_s = x_p.shape[0] // tm, x_s.shape[0] // tm
    n = x_p.shape[0] + x_s.shape[0]
    row = lambda w: pl.BlockSpec((tm, w), lambda i: (i, 0))
    full = lambda a: pl.BlockSpec(a.shape, lambda i: (0, 0))
    return pl.pallas_call(
        functools.partial(_wo_ln_body, alpha=alpha, tiles_p=tiles_p),
        grid=(tiles_p + tiles_s,),
        in_specs=[*_group_specs(tm, d, tiles_p), *_group_specs(tm, of_p.shape[1], tiles_p),
                  *_group_specs(tm, or_p.shape[1], tiles_p), full(w1), full(w2), full(g), full(b)],
        out_specs=[row(d), row(d)],
        out_shape=[jax.ShapeDtypeStruct((n, d), F32), jax.ShapeDtypeStruct((n, d), BF16)],
        compiler_params=_params("arbitrary"),
        name="wo_ln1",
    )(x_p, x_s, of_p, of_s, or_p, or_s, w1, w2, g, b)


def _top_desc(s, count, n_masked, one_per_round):
    n = s.shape[0]

    vals, work = [], s
    if one_per_round:
        iota = lax.broadcasted_iota(jnp.int32, s.shape, 0)
        for _ in range(count):
            mj = jnp.max(work, axis=0, keepdims=True)
            vals.append(mj)
            first = jnp.min(jnp.where(work == mj, iota, n), axis=0, keepdims=True)
            work = jnp.where(iota == first, NEG_INF, work)
        return vals, jnp.zeros((1, s.shape[1]), F32)
    for _ in range(count):
        mj = jnp.max(work, axis=0, keepdims=True)
        vals.append(mj)
        work = jnp.where(work == mj, NEG_INF, work)
    removed = jnp.sum(jnp.where(work == NEG_INF, 1.0, 0.0), axis=0, keepdims=True)
    return vals, jnp.abs(removed - float(count + n_masked))


_CAND = [(a, b) for a in range(PEER_TOPK + 1) for b in range(PEER_TOPK + 1) if (a + 1) * (b + 1) <= PEER_TOPK + 1]


def _peer_score_body(hb_ref, wq_ref, keys_ref, t1_ref, e1_ref, e2_ref, q_sc):
    tm = hb_ref.shape[0]
    q_sc[...] = jnp.dot(hb_ref[...], wq_ref[...], preferred_element_type=F32).astype(BF16)

    def head(h, one_per_round):
        s = []
        for c in range(2):
            col = (2 * h + c) * PEER_HALF
            s.append(lax.dot_general(keys_ref[h, c], q_sc[:, col:col + PEER_HALF], _NT,
                                     preferred_element_type=F32))
        v1, bad1 = _top_desc(s[0], PEER_TOPK + 1, 0, one_per_round)
        v2, bad2 = _top_desc(s[1], PEER_TOPK + 1, 0, one_per_round)
        rows = [v1[a] + v2[b] for a, b in _CAND]
        pad = (-len(rows)) % 8
        cand = jnp.concatenate(rows + [jnp.full((pad, tm), NEG_INF, F32)], axis=0)
        top, bad3 = _top_desc(cand, PEER_TOPK + 1, pad, one_per_round)
        z = jnp.ones((1, tm), F32)
        for j in range(1, PEER_TOPK):
            z = z + jnp.exp(top[j] - top[0])
        theta = 0.5 * (top[PEER_TOPK - 1] + top[PEER_TOPK])
        c1 = s[0] - v1[0]
        t1_ref[h] = jnp.exp((theta - top[0]) - c1)
        e1_ref[h] = jnp.exp(c1 - jnp.log(z))
        e2_ref[h] = jnp.exp(s[1] - v2[0])
        return jnp.max(bad1 + bad2 + bad3)

    bad = [head(h, False) for h in range(PEER_HEADS)]
    for h in range(PEER_HEADS):
        @pl.when(bad[h] > 0.0)
        def _():
            head(h, True)


def _peer_scores(hb, wq, keys, tm):
    n, d = hb.shape
    return pl.pallas_call(
        _peer_score_body,
        grid=(n // tm,),
        in_specs=[pl.BlockSpec((tm, d), lambda i: (i, 0)), pl.BlockSpec(wq.shape, lambda i: (0, 0)),
                  pl.BlockSpec(keys.shape, lambda i: (0, 0, 0, 0))],
        out_specs=[pl.BlockSpec((PEER_HEADS, PEER_KEYS, tm), lambda i: (0, 0, i))] * 3,
        out_shape=[jax.ShapeDtypeStruct((PEER_HEADS, PEER_KEYS, n), F32)] * 3,
        scratch_shapes=[pltpu.VMEM((tm, wq.shape[1]), BF16)],
        compiler_params=_params("parallel"),
        name="peer_scores",
    )(hb, wq, keys)


PEER_SUB = 256


PEER_EB = 2 * PEER_SUB


def _peer_dense_body(hb_ref, u_ref, v0_ref, vp_ref, t1_ref, e1_ref, e2_ref, o_ref, act_sc, w_sc, acc_sc):
    j = pl.program_id(1)
    n_steps = pl.num_programs(1) - 1
    tm = hb_ref.shape[0]
    n_i1 = PEER_SUB // PEER_KEYS

    @pl.when(j == 0)
    def _():
        o_ref[...] = jnp.zeros(o_ref.shape, F32)
        acc_sc[...] = jnp.zeros(acc_sc.shape, F32)
        w_sc[1] = jnp.zeros(w_sc.shape[1:], BF16)

    d_model = o_ref.shape[1]
    n_tiles = tm // 128
    col_chunks = [slice(q * 256, (q + 1) * 256) for q in range(d_model // 256)]

    def product(slot, v_ref, cols):
        dst = acc_sc if slot == 1 else o_ref
        dst[:, cols] += jnp.dot(w_sc[slot], v_ref[:, cols].astype(BF16), preferred_element_type=F32)

    def build_w(r, prev_slot, prev_v):
        act = lax.dot_general(hb_ref[...], u_ref[r * PEER_SUB:(r + 1) * PEER_SUB, :].astype(BF16), _NT,
                              preferred_element_type=F32)
        act_sc[r] = 0.5 * act * (1.0 + lax.erf(act * (2.0 ** -0.5)))
        i1_0 = (2 * j + r) * n_i1
        t1_rows = [[t1_ref[h, pl.ds(i1_0 + k, 1), :] for k in range(n_i1)] for h in range(PEER_HEADS)]
        e1_rows = [[e1_ref[h, pl.ds(i1_0 + k, 1), :] for k in range(n_i1)] for h in range(PEER_HEADS)]
        pending_chunks = list(col_chunks)
        for c in range(n_tiles):
            lanes = slice(c * 128, (c + 1) * 128)
            for k in range(n_i1):
                gate = None
                for h in range(PEER_HEADS):
                    e2 = e2_ref[h, :, lanes]
                    g = jnp.where(e2 >= t1_rows[h][k][:, lanes], e2, 0.0) * e1_rows[h][k][:, lanes]
                    gate = g if gate is None else gate + g
                cols = slice(k * PEER_KEYS, (k + 1) * PEER_KEYS)
                w_sc[r, lanes, cols] = (act_sc[r, lanes, cols] * gate.T).astype(BF16)
            n_now = -(-len(pending_chunks) // (n_tiles - c))
            for _ in range(n_now):
                product(prev_slot, prev_v, pending_chunks.pop(0))

    @pl.when(j < n_steps)
    def _main():
        build_w(0, 1, vp_ref)
        build_w(1, 0, v0_ref)

    @pl.when(j == n_steps)
    def _drain():
        for cols in col_chunks:
            product(1, vp_ref, cols)
        o_ref[...] += acc_sc[...]


def _peer_dense(hb, u, v, t1, e1, e2, tm):
    n, d = hb.shape
    n_steps = u.shape[0] // PEER_EB
    last = n_steps - 1
    per_tile = lambda shape, imap: pl.BlockSpec(shape, imap, pipeline_mode=pl.Buffered(1))
    gate_spec = lambda: per_tile((PEER_HEADS, PEER_KEYS, tm), lambda i, j: (0, 0, i))
    return pl.pallas_call(
        _peer_dense_body,
        grid=(n // tm, n_steps + 1),
        in_specs=[
            per_tile((tm, d), lambda i, j: (i, 0)),
            pl.BlockSpec((PEER_EB, d), lambda i, j: (jnp.minimum(j, last), 0)),
            pl.BlockSpec((PEER_SUB, d), lambda i, j: (2 * jnp.minimum(j, last), 0)),
            pl.BlockSpec((PEER_SUB, d), lambda i, j: (jnp.maximum(2 * j - 1, 0), 0)),
            gate_spec(), gate_spec(), gate_spec(),
        ],
        out_specs=pl.BlockSpec((tm, d), lambda i, j: (i, 0)),
        out_shape=jax.ShapeDtypeStruct((n, d), F32),
        scratch_shapes=[pltpu.VMEM((2, tm, PEER_SUB), F32), pltpu.VMEM((2, tm, PEER_SUB), BF16),
                        pltpu.VMEM((tm, d), F32)],
        compiler_params=_params("parallel", "arbitrary"),
        name="peer_dense",
    )(hb, u, v, v, t1, e1, e2)


def _out_body(h_ref, po_ref, pp_ref, ps_ref, g_ref, b_ref, wg_ref, bg_ref, we_ref, yp_ref, ys_ref,
              *, alpha, tiles_p):
    i = pl.program_id(0)
    h2 = _layer_norm(alpha * h_ref[...] + po_ref[...], g_ref[...], b_ref[...])
    lin = jnp.dot(h2.astype(BF16), wg_ref[...], preferred_element_type=F32) + bg_ref[...]
    p_emb = jnp.where(i < tiles_p, pp_ref[...], ps_ref[...])
    emb = jnp.dot(p_emb.astype(BF16), we_ref[...], preferred_element_type=F32)
    y = h2 + emb / (1.0 + jnp.exp(-lin))

    @pl.when(i < tiles_p)
    def _():
        yp_ref[...] = y

    @pl.when(i >= tiles_p)
    def _():
        ys_ref[...] = y


def _out_stage(h1, peer_out, p_p, p_s, g, b, wg, bg, we, alpha, tm):
    n, d = h1.shape
    tiles_p, tiles_s = p_p.shape[0] // tm, p_s.shape[0] // tm
    assert (tiles_p + tiles_s) * tm == n
    row = lambda w: pl.BlockSpec((tm, w), lambda i: (i, 0))
    full = lambda a: pl.BlockSpec(a.shape, lambda i: (0, 0))
    return pl.pallas_call(
        functools.partial(_out_body, alpha=alpha, tiles_p=tiles_p),
        grid=(tiles_p + tiles_s,),
        in_specs=[row(d), row(d), *_group_specs(tm, p_p.shape[1], tiles_p),
                  full(g), full(b), full(wg), full(bg), full(we)],
        out_specs=list(_group_specs(tm, d, tiles_p)),
        out_shape=[jax.ShapeDtypeStruct((tiles_p * tm, d), F32), jax.ShapeDtypeStruct((tiles_s * tm, d), F32)],
        compiler_params=_params("arbitrary"),
        name="ln2_gate_out",
    )(h1, peer_out, p_p, p_s, g, b, wg, bg, we)


def kernel(x_prompt, x_sample, cache_k, cache_v, cache_logf, state_ret, page_table, p_prompt, p_sample, w_in, b_f,
           gn_g, w_o, ln1_g, ln1_b, w_pq, peer_keys, peer_u, peer_v, ln2_g, ln2_b, w_pg, b_pg, w_pe):
    depth = w_in.shape[0]
    assert depth == 1, "one layer"
    batch, seq, d_model = x_prompt.shape
    db, t_dec, _ = x_sample.shape
    n_pool = cache_k.shape[1]
    n_fox = cache_k.shape[3]
    n_heads = d_model // HEAD_DIM
    n_ret = n_heads - n_fox
    w_fox = n_fox * HEAD_DIM
    w_ret = n_ret * HEAD_DIM
    n_pages = page_table.shape[1]
    past_len = n_pages * PAGE_SIZE
    np_tok = batch * seq
    ns_tok = db * t_dec
    n_tok = np_tok + ns_tok
    alpha = (2.0 * depth) ** 0.25
    assert cache_k.shape[2] == PAGE_SIZE and cache_k.shape[4] == HEAD_DIM and t_dec % 8 == 0

    tm_big = _pick(n_tok, (768, 512, 384, 256, 128))
    tm_mid = _pick(math.gcd(np_tok, ns_tok), (256, 128))

    wi = w_in[0]
    f0 = 3 * w_fox
    w_main = jnp.concatenate([wi[:, :f0], wi[:, f0 + n_fox:]], axis=1).astype(BF16)
    wf = wi[:, f0:f0 + n_fox].astype(BF16)
    wo1 = w_o[0, :w_fox].astype(BF16)
    wo2 = w_o[0, w_fox:].astype(BF16)
    wq = w_pq[0].astype(BF16)
    keys = peer_keys[0].astype(BF16)
    u_b = peer_u[0]
    v_b = peer_v[0]
    wg = w_pg[0].astype(BF16)
    we = w_pe[0].astype(BF16)
    row = lambda a: a.reshape(1, -1)

    x_p = x_prompt.reshape(np_tok, d_model)
    x_s = x_sample.reshape(ns_tok, d_model)
    b_col = b_f[0].reshape(n_fox, 1)

    qf_p, kf_p, vf_p, zr_p, lf_p = _in_proj(x_p, w_main, wf, b_col, w_fox)
    qf_s, kf_s, vf_s, zr_s, lf_s = _in_proj(x_s, w_main, wf, b_col, w_fox)

    c_t = _segment_cumsum(lf_p, seq)
    o_fox_p = _fox_prompt(qf_p, kf_p, vf_p, c_t, batch, seq, n_fox)

    cn_t = _segment_cumsum(jnp.pad(lf_s, ((0, 0), (0, (-ns_tok) % 128))), t_dec)[:, :ns_tok]
    cn = cn_t.reshape(n_fox, db, t_dec).transpose(1, 0, 2)
    cn_pad = jnp.pad(cn, ((0, 0), (0, 0), (0, PAGE_SIZE - t_dec)), constant_values=BIG)
    page_et = _page_suffix(cache_logf[0].reshape(n_pool, PAGE_SIZE * n_fox))
    o_fox_s = _fox_sample(qf_s, kf_s, vf_s, cn_pad, cache_k, cache_v, page_et, page_table, t_dec, n_fox)

    gn_row = row(gn_g[0])
    cos_p, sin_p = _rotary_tables(jnp.arange(seq, dtype=F32))
    cos_s, sin_s = _rotary_tables(past_len + jnp.arange(t_dec, dtype=F32))
    zeros_state = jnp.zeros((batch, n_ret, HEAD_DIM, HEAD_DIM), F32)
    o_ret_p, s_p = _retention(zr_p, 0, 0, n_ret, batch, seq, RET_CHUNK, zeros_state, cos_p, sin_p,
                              _retention_tables(n_ret, RET_CHUNK), gn_row)
    o_ret_s, s_s = _retention(zr_s, 0, 0, n_ret, db, t_dec, t_dec, state_ret[0], cos_s, sin_s,
                              _retention_tables(n_ret, t_dec), gn_row)

    h1, h1b = _wo_ln(x_p, x_s, o_fox_p, o_fox_s, o_ret_p, o_ret_s, wo1, wo2, row(ln1_g[0]), row(ln1_b[0]),
                     alpha, tm_mid)
    t1, e1, e2 = _peer_scores(h1b, wq, keys, _pick(n_tok, (128,)))
    peer_out = _peer_dense(h1b, u_b, v_b, t1, e1, e2, tm_big)
    y_p, y_s = _out_stage(h1, peer_out, p_prompt[0].reshape(np_tok, -1), p_sample[0].reshape(ns_tok, -1),
                          row(ln2_g[0]), row(ln2_b[0]), wg, row(b_pg[0]), we, alpha, tm_mid)

    heads_p = lambda a: a.reshape(1, batch, seq, n_fox, HEAD_DIM)
    heads_s = lambda a: a.reshape(1, db, t_dec, n_fox, HEAD_DIM)
    return (y_p.reshape(batch, seq, d_model), y_s.reshape(db, t_dec, d_model),
            heads_p(kf_p), heads_p(vf_p), lf_p.T.reshape(1, batch, seq, n_fox), s_p[None],
            heads_s(kf_s), heads_s(vf_s), lf_s.T.reshape(1, db, t_dec, n_fox), s_s[None])
```

```python
import functools
import math

import numpy as np
import jax
import jax.numpy as jnp
from jax import lax
from jax.experimental import pallas as pl
from jax.experimental.pallas import tpu as pltpu

F32 = jnp.float32
BF16 = jnp.bfloat16

HEAD_DIM = 128
PAGE_SIZE = 128
ROPE_BASE = 10000.0
RET_CHUNK = 128
PEER_HEADS = 8
PEER_KEYS = 128
PEER_HALF = 128
PEER_TOPK = 16
LN_EPS = 1e-5
GN_EPS = 1e-6
NEG_INF = float("-inf")
BIG = 1e30

VMEM_LIMIT = 56 * 1024 * 1024

_NT = (((1,), (1,)), ((), ()))
_TN = (((0,), (0,)), ((), ()))


def _pick(n, candidates):
    for c in candidates:
        if n % c == 0:
            return c
    raise ValueError(f"no tile in {candidates} divides {n}")


def _idiv(x, c):
    assert c & (c - 1) == 0
    return lax.shift_right_logical(x, int(math.log2(c)))


def _imod(x, c):
    assert c & (c - 1) == 0
    return x & (c - 1)


def _params(*sem):
    return pltpu.CompilerParams(dimension_semantics=sem, vmem_limit_bytes=VMEM_LIMIT)


def _in_proj_body(x_ref, w_ref, wf_ref, b_ref, q_ref, k_ref, v_ref, zr_ref, lf_ref, xb_sc):
    j = pl.program_id(1)

    @pl.when(j == 0)
    def _():
        xb_sc[...] = x_ref[...].astype(BF16)

    xb = xb_sc[...]
    r = jnp.dot(xb, w_ref[...], preferred_element_type=F32)

    @pl.when(j == 0)
    def _():
        q_ref[...] = r
        f = lax.dot_general(wf_ref[...], xb, (((0,), (1,)), ((), ())),
                            preferred_element_type=F32) + b_ref[...]
        lf_ref[...] = jnp.minimum(f, 0.0) - jnp.log1p(jnp.exp(-jnp.abs(f)))

    @pl.when(j == 1)
    def _():
        k_ref[...] = r

    @pl.when(j == 2)
    def _():
        v_ref[...] = r

    @pl.when(j >= 3)
    def _():
        zr_ref[...] = r


def _in_proj(x, w_main, wf, b_col, w_fox):
    m, d = x.shape
    n_cols = w_main.shape[1]
    n_h = wf.shape[1]
    tm = _pick(m, (512, 256, 128))
    nj = n_cols // w_fox
    assert nj * w_fox == n_cols and nj > 3
    head = lambda: pl.BlockSpec((tm, w_fox), lambda i, j: (i, 0))
    return pl.pallas_call(
        _in_proj_body,
        grid=(m // tm, nj),
        in_specs=[pl.BlockSpec((tm, d), lambda i, j: (i, 0)), pl.BlockSpec((d, w_fox), lambda i, j: (0, j)),
                  pl.BlockSpec((d, n_h), lambda i, j: (0, 0)), pl.BlockSpec((n_h, 1), lambda i, j: (0, 0))],
        out_specs=[head(), head(), head(),
                   pl.BlockSpec((tm, w_fox), lambda i, j: (i, jnp.maximum(j - 3, 0))),
                   pl.BlockSpec((n_h, tm), lambda i, j: (0, i))],
        out_shape=[jax.ShapeDtypeStruct((m, w_fox), F32)] * 3
        + [jax.ShapeDtypeStruct((m, n_cols - 3 * w_fox), F32), jax.ShapeDtypeStruct((n_h, m), F32)],
        scratch_shapes=[pltpu.VMEM((tm, d), BF16)],
        compiler_params=_params("parallel", "arbitrary"),
        name="in_proj",
    )(x, w_main, wf, b_col)


def _split3(x):
    hi = x.astype(BF16)
    r1 = x - hi.astype(F32)
    mid = r1.astype(BF16)
    lo = (r1 - mid.astype(F32)).astype(BF16)
    return hi, mid, lo


def _dot3(parts, m):
    acc = jnp.dot(parts[0], m, preferred_element_type=F32)
    acc = acc + jnp.dot(parts[1], m, preferred_element_type=F32)
    return acc + jnp.dot(parts[2], m, preferred_element_type=F32)


def _cumsum_body(x_ref, o_ref, *, seg, blk):
    rows, length = x_ref.shape
    r = lax.broadcasted_iota(jnp.int32, (blk, blk), 0)
    c = lax.broadcasted_iota(jnp.int32, (blk, blk), 1)
    if seg >= blk:
        tri = (r <= c)
    else:
        tri = (r <= c) & (_idiv(r, seg) == _idiv(c, seg))
    tri = jnp.where(tri, 1.0, 0.0).astype(BF16)
    carry = jnp.zeros((rows, 1), F32)
    for j in range(length // blk):
        if seg < blk or (j * blk) % seg == 0:
            carry = jnp.zeros((rows, 1), F32)
        cs = _dot3(_split3(x_ref[:, j * blk:(j + 1) * blk]), tri) + carry
        o_ref[:, j * blk:(j + 1) * blk] = cs
        carry = cs[:, blk - 1:blk]


def _segment_cumsum(x, seg):
    rows, length = x.shape
    blk = _pick(length, (256, 128))
    assert seg % blk == 0 or blk % seg == 0
    return pl.pallas_call(
        functools.partial(_cumsum_body, seg=seg, blk=blk),
        out_shape=jax.ShapeDtypeStruct((rows, length), F32),
        compiler_params=pltpu.CompilerParams(vmem_limit_bytes=VMEM_LIMIT),
        name="segment_cumsum",
    )(x)


def _fox_prompt_body(q_ref, k_ref, v_ref, c_ref, o_ref, *, tq, scale):
    s_len = q_ref.shape[0]
    n_h = q_ref.shape[1] // HEAD_DIM
    row = lax.broadcasted_iota(jnp.int32, (tq, tq), 0)
    col = lax.broadcasted_iota(jnp.int32, (tq, tq), 1)
    causal = col <= row

    def step(qs, kj, carry, mask):
        ks = kj * tq if isinstance(kj, int) else pl.multiple_of(kj * tq, tq)
        out = []
        for h in range(n_h):
            m, l, acc = carry[h]
            lanes = slice(h * HEAD_DIM, (h + 1) * HEAD_DIM)
            k = k_ref[pl.ds(ks, tq), lanes].astype(BF16)
            v = v_ref[pl.ds(ks, tq), lanes].astype(BF16)
            s = lax.dot_general(qs[h], k, _NT, preferred_element_type=F32) - c_ref[h, kj]
            if mask:
                s = jnp.where(causal, s, NEG_INF)
            m_new = jnp.maximum(m, jnp.max(s, axis=1, keepdims=True))
            p = jnp.exp(s - m_new)
            alpha = jnp.exp(m - m_new)
            l = alpha * l + jnp.sum(p, axis=1, keepdims=True)
            acc = alpha * acc + jnp.dot(p.astype(BF16), v, preferred_element_type=F32)
            out.append((m_new, l, acc))
        return tuple(out)

    for qi in range(s_len // tq):
        qs = [(q_ref[qi * tq:(qi + 1) * tq, h * HEAD_DIM:(h + 1) * HEAD_DIM] * scale).astype(BF16)
              for h in range(n_h)]
        carry = tuple((jnp.full((tq, 1), NEG_INF, F32), jnp.zeros((tq, 1), F32), jnp.zeros((tq, HEAD_DIM), F32))
                      for _ in range(n_h))
        if qi > 0:
            carry = lax.fori_loop(0, qi, lambda kj, cr: step(qs, kj, cr, False), carry)
        carry = step(qs, qi, carry, True)
        for h in range(n_h):
            m, l, acc = carry[h]
            o_ref[qi * tq:(qi + 1) * tq, h * HEAD_DIM:(h + 1) * HEAD_DIM] = (acc / l).astype(o_ref.dtype)


FOX_HEADS_PER_STEP = 4


def _fox_prompt(q, k, v, c_t, batch, seq, n_fox):
    tq = _pick(seq, (512, 256, 128))
    hp = FOX_HEADS_PER_STEP
    assert n_fox % hp == 0
    c4 = c_t.reshape(n_fox, batch * seq // tq, 1, tq)
    nb = seq // tq
    groups = n_fox // hp
    width = hp * HEAD_DIM
    return pl.pallas_call(
        functools.partial(_fox_prompt_body, tq=tq, scale=HEAD_DIM ** -0.5),
        grid=(batch, groups),
        in_specs=[
            pl.BlockSpec((seq, width), lambda b, h: (b, h)),
            pl.BlockSpec((seq, width), lambda b, h: (b, h)),
            pl.BlockSpec((seq, width), lambda b, h: (b, h)),
            pl.BlockSpec((hp, nb, 1, tq), lambda b, h: (h, b, 0, 0)),
        ],
        out_specs=pl.BlockSpec((seq, width), lambda b, h: (b, h)),
        out_shape=jax.ShapeDtypeStruct((batch * seq, n_fox * HEAD_DIM), BF16),
        compiler_params=_params("parallel", "parallel"),
        name="fox_prompt",
    )(q, k, v, c4)


def _page_suffix_body(lp_ref, et_ref):
    n = lp_ref.shape[1]
    n_h = n // PAGE_SIZE
    r = lax.broadcasted_iota(jnp.int32, (n, n), 0)
    c = lax.broadcasted_iota(jnp.int32, (n, n), 1)
    same_head = _imod(r, n_h) == _imod(c, n_h)
    later = _idiv(r, n_h) > _idiv(c, n_h)
    m_e = jnp.where(same_head & later, 1.0, 0.0).astype(BF16)
    m_t = jnp.where(same_head, 1.0, 0.0).astype(BF16)
    parts = _split3(lp_ref[...])
    et_ref[:, 0, :] = _dot3(parts, m_e)
    et_ref[:, 1, :] = _dot3(parts, m_t)


def _page_suffix(lp_rows):
    n_pool, n = lp_rows.shape
    tr = _pick(n_pool, (512, 256, 128, 64, 32, 16, 8))
    return pl.pallas_call(
        _page_suffix_body,
        grid=(n_pool // tr,),
        in_specs=[pl.BlockSpec((tr, n), lambda i: (i, 0))],
        out_specs=pl.BlockSpec((tr, 2, n), lambda i: (i, 0, 0)),
        out_shape=jax.ShapeDtypeStruct((n_pool, 2, n), F32),
        compiler_params=_params("parallel"),
        name="page_suffix",
    )(lp_rows)


def _rows_per_head(x, t):
    h, l = x.shape
    return jnp.broadcast_to(x[:, None, :], (h, t, l)).reshape(h * t, l)


def _fox_sample_body(pt_ref, q_ref, kn_ref, vn_ref, cn_ref, *refs, scale, n_fox, n_grp):
    k_refs = refs[:n_grp]
    v_refs = refs[n_grp:2 * n_grp]
    et_refs = refs[2 * n_grp:3 * n_grp]
    o_ref, q_sc, hm_sc, m_sc, l_sc, acc, carry = refs[3 * n_grp:]
    p = pl.program_id(1)
    t = q_ref.shape[0]
    rows = n_fox * t
    keys = PAGE_SIZE * n_fox

    @pl.when(p == 0)
    def _init():
        q_sc[...] = (jnp.concatenate([q_ref[:, h * HEAD_DIM:(h + 1) * HEAD_DIM] for h in range(n_fox)], axis=0)
                     * scale).astype(BF16)
        r = lax.broadcasted_iota(jnp.int32, (rows, keys), 0)
        c = lax.broadcasted_iota(jnp.int32, (rows, keys), 1)
        hm_sc[...] = jnp.where(_idiv(r, t) == _imod(c, n_fox), 0.0, NEG_INF)
        m_sc[...] = jnp.full(m_sc.shape, NEG_INF, F32)
        l_sc[...] = jnp.zeros(l_sc.shape, F32)
        acc[...] = jnp.zeros(acc.shape, F32)
        carry[...] = jnp.zeros(carry.shape, F32)

    q = q_sc[...]
    hm = hm_sc[...]
    c = carry[...]
    scores = []
    for g in range(n_grp):
        et = et_refs[g][...]
        kf = k_refs[g][...].reshape(keys, HEAD_DIM).astype(BF16)
        scores.append(lax.dot_general(q, kf, _NT, preferred_element_type=F32) + (et[0:1] + c) + hm)
        c = c + et[1:2]
    carry[...] = c
    m_old = m_sc[...]
    m_new = m_old
    for s in scores:
        m_new = jnp.maximum(m_new, jnp.max(s, axis=1, keepdims=True))
    alpha = jnp.exp(m_old - m_new)
    l_new = alpha * l_sc[...]
    acc_new = alpha * acc[...]
    for g in range(n_grp):
        pr = jnp.exp(scores[g] - m_new)
        l_new = l_new + jnp.sum(pr, axis=1, keepdims=True)
        vf = v_refs[g][...].reshape(keys, HEAD_DIM).astype(BF16)
        acc_new = acc_new + jnp.dot(pr.astype(BF16), vf, preferred_element_type=F32)
    m_sc[...] = m_new
    l_sc[...] = l_new
    acc[...] = acc_new

    @pl.when(p == pl.num_programs(1) - 1)
    def _fin():
        width = n_fox * HEAD_DIM
        r = lax.broadcasted_iota(jnp.int32, (rows, width), 0)
        cw = lax.broadcasted_iota(jnp.int32, (rows, width), 1)
        q_rep = jnp.concatenate([q_ref[...]] * n_fox, axis=0) * scale
        qbd = jnp.where(_idiv(r, t) == _idiv(cw, HEAD_DIM), q_rep, 0.0).astype(BF16)
        pad = jnp.zeros((PAGE_SIZE - t, width), BF16)
        k_new = jnp.concatenate([kn_ref[...].astype(BF16), pad], axis=0)
        v_new = jnp.concatenate([vn_ref[...].astype(BF16), pad], axis=0)
        s = lax.dot_general(qbd, k_new, _NT, preferred_element_type=F32) - _rows_per_head(cn_ref[...], t)
        rr = lax.broadcasted_iota(jnp.int32, (rows, PAGE_SIZE), 0)
        cc = lax.broadcasted_iota(jnp.int32, (rows, PAGE_SIZE), 1)
        s = jnp.where(cc <= _imod(rr, t), s, NEG_INF)
        m_fin = jnp.maximum(m_new, jnp.max(s, axis=1, keepdims=True))
        pr = jnp.exp(s - m_fin)
        a_fin = jnp.exp(m_new - m_fin)
        l_fin = a_fin * l_new + jnp.sum(pr, axis=1, keepdims=True)
        o_wide = jnp.dot(pr.astype(BF16), v_new, preferred_element_type=F32)
        for h in range(n_fox):
            blk = slice(h * t, (h + 1) * t)
            lanes = slice(h * HEAD_DIM, (h + 1) * HEAD_DIM)
            o_h = a_fin[blk] * acc_new[blk] + o_wide[blk, lanes]
            o_ref[:, lanes] = (o_h / l_fin[blk]).astype(o_ref.dtype)


def _fox_sample(q, k_new, v_new, cn_pad, cache_k, cache_v, page_et, page_table, t, n_fox):
    db, n_pages = page_table.shape
    width = n_fox * HEAD_DIM
    keys = PAGE_SIZE * n_fox
    n_grp = _pick(n_pages, (16, 8, 4, 2, 1))
    last = n_pages - 1

    def kv_spec(g):
        return pl.BlockSpec((None, None, PAGE_SIZE, n_fox, HEAD_DIM),
                            lambda b, p, pt: (0, pt[b, last - p * n_grp - g], 0, 0, 0))

    def et_spec(g):
        return pl.BlockSpec((None, 2, keys), lambda b, p, pt: (pt[b, last - p * n_grp - g], 0, 0))

    rng = range(n_grp)
    grid_spec = pltpu.PrefetchScalarGridSpec(
        num_scalar_prefetch=1,
        grid=(db, n_pages // n_grp),
        in_specs=[
            pl.BlockSpec((t, width), lambda b, p, pt: (b, 0)),
            pl.BlockSpec((t, width), lambda b, p, pt: (b, 0)),
            pl.BlockSpec((t, width), lambda b, p, pt: (b, 0)),
            pl.BlockSpec((None, n_fox, PAGE_SIZE), lambda b, p, pt: (b, 0, 0)),
        ] + [kv_spec(g) for g in rng] + [kv_spec(g) for g in rng] + [et_spec(g) for g in rng],
        out_specs=pl.BlockSpec((t, width), lambda b, p, pt: (b, 0)),
        scratch_shapes=[
            pltpu.VMEM((n_fox * t, HEAD_DIM), BF16),
            pltpu.VMEM((n_fox * t, keys), F32),
            pltpu.VMEM((n_fox * t, 1), F32),
            pltpu.VMEM((n_fox * t, 1), F32),
            pltpu.VMEM((n_fox * t, HEAD_DIM), F32),
            pltpu.VMEM((1, keys), F32),
        ],
    )
    return pl.pallas_call(
        functools.partial(_fox_sample_body, scale=HEAD_DIM ** -0.5, n_fox=n_fox, n_grp=n_grp),
        grid_spec=grid_spec,
        out_shape=jax.ShapeDtypeStruct((db * t, width), F32),
        compiler_params=_params("parallel", "arbitrary"),
        name="fox_sample",
    )(page_table, q, k_new, v_new, cn_pad, *([cache_k] * n_grp), *([cache_v] * n_grp), *([page_et] * n_grp))


def _rotate(x, cos_f, sin_f):
    return x * cos_f + pltpu.roll(x, HEAD_DIM // 2, 1) * sin_f


def _retention_body(q_ref, k_ref, v_ref, g_ref, cos_ref, sin_ref, dec_ref, qd_ref, kd_ref, cd_ref, gn_ref,
                    s0_ref, o_ref, sout_ref, state, *, k_scale):
    c = pl.program_id(1)
    n_ret = state.shape[0]

    @pl.when(c == 0)
    def _():
        state[...] = s0_ref[...]

    cos_f = cos_ref[...]
    sin_f = sin_ref[...]
    for h in range(n_ret):
        lanes = slice(h * HEAD_DIM, (h + 1) * HEAD_DIM)
        q = _rotate(q_ref[:, lanes], cos_f, sin_f)
        k = _rotate(k_ref[:, lanes], cos_f, sin_f) * k_scale
        qb = q.astype(BF16)
        vb = v_ref[:, lanes].astype(BF16)
        s_in = state[h]
        scores = lax.dot_general(qb, k.astype(BF16), _NT, preferred_element_type=F32) * dec_ref[h]
        y = jnp.dot(scores.astype(BF16), vb, preferred_element_type=F32)
        y = y + jnp.dot(qb, s_in.astype(BF16), preferred_element_type=F32) * qd_ref[h]
        kd = (k * kd_ref[h]).astype(BF16)
        new_state = cd_ref[h] * s_in + lax.dot_general(kd, vb, _TN, preferred_element_type=F32)
        state[h] = new_state
        mu = jnp.mean(y, axis=1, keepdims=True)
        yc = y - mu
        var = jnp.mean(yc * yc, axis=1, keepdims=True)
        yn = yc * lax.rsqrt(var + GN_EPS) * gn_ref[:, lanes]
        g = g_ref[:, lanes]
        o_ref[:, lanes] = (g / (1.0 + jnp.exp(-g)) * yn).astype(o_ref.dtype)

    @pl.when(c == pl.num_programs(1) - 1)
    def _():
        sout_ref[...] = state[...]


def _retention(z, row0, col0, n_ret, batch, seq, chunk, state0, cos_f, sin_f, tables, gn_row):
    dec, qd, kd, cd = tables
    nc = seq // chunk
    rb0 = row0 // chunk
    d = HEAD_DIM
    width = n_ret * d
    assert col0 % width == 0
    cb0 = col0 // width
    zspec = lambda off: pl.BlockSpec((chunk, width), lambda b, c: (rb0 + b * nc + c, cb0 + off))
    full = lambda a: pl.BlockSpec(a.shape, lambda b, c: (0,) * a.ndim)
    out, s_fin = pl.pallas_call(
        functools.partial(_retention_body, k_scale=HEAD_DIM ** -0.5),
        grid=(batch, nc),
        in_specs=[
            zspec(0), zspec(1), zspec(2), zspec(3),
            pl.BlockSpec((chunk, d), lambda b, c: (c, 0)),
            pl.BlockSpec((chunk, d), lambda b, c: (c, 0)),
            full(dec), full(qd), full(kd), full(cd), full(gn_row),
            pl.BlockSpec((None, n_ret, d, d), lambda b, c: (b, 0, 0, 0)),
        ],
        out_specs=[
            pl.BlockSpec((chunk, width), lambda b, c: (b * nc + c, 0)),
            pl.BlockSpec((None, n_ret, d, d), lambda b, c: (b, 0, 0, 0)),
        ],
        out_shape=[
            jax.ShapeDtypeStruct((batch * seq, width), BF16 if chunk % 16 == 0 else F32),
            jax.ShapeDtypeStruct((batch, n_ret, d, d), F32),
        ],
        scratch_shapes=[pltpu.VMEM((n_ret, d, d), F32)],
        compiler_params=_params("parallel", "arbitrary"),
        name=f"retention_{chunk}",
    )(z, z, z, z, cos_f, sin_f, dec, qd, kd, cd, gn_row, state0)
    return out, s_fin


def _retention_tables(n_ret, chunk):
    log_gamma = jnp.log1p(-jnp.exp2(-5.0 - jnp.arange(n_ret, dtype=F32)))
    pos = jnp.arange(chunk, dtype=F32)
    diff = pos[:, None] - pos[None, :]
    dec = jnp.where((diff >= 0)[None], jnp.exp(jnp.maximum(diff, 0.0)[None] * log_gamma[:, None, None]), 0.0)
    qd = jnp.exp((pos[None, :] + 1.0) * log_gamma[:, None])
    kd = jnp.exp((chunk - 1.0 - pos)[None, :] * log_gamma[:, None])
    cd = jnp.exp(chunk * log_gamma)
    bc = lambda a: jnp.broadcast_to(a[:, :, None], (n_ret, chunk, HEAD_DIM))
    return dec, bc(qd), bc(kd), jnp.broadcast_to(cd[:, None, None], (n_ret, 1, HEAD_DIM))


def _rotary_tables(pos):
    half = HEAD_DIM // 2
    inv_freq = ROPE_BASE ** (-jnp.arange(half, dtype=F32) / half)
    ang = pos[:, None] * inv_freq[None, :]
    cos, sin = jnp.cos(ang), jnp.sin(ang)
    return jnp.concatenate([cos, cos], -1), jnp.concatenate([-sin, sin], -1)


def _layer_norm(x, g, b):
    mu = jnp.mean(x, axis=1, keepdims=True)
    xc = x - mu
    var = jnp.mean(xc * xc, axis=1, keepdims=True)
    return xc * lax.rsqrt(var + LN_EPS) * g + b


def _group_specs(tm, width, tiles_p):
    return (pl.BlockSpec((tm, width), lambda i: (jnp.minimum(i, tiles_p - 1), 0)),
            pl.BlockSpec((tm, width), lambda i: (jnp.maximum(i - tiles_p, 0), 0)))


def _wo_ln_body(xp_ref, xs_ref, ofp_ref, ofs_ref, orp_ref, ors_ref, w1_ref, w2_ref, g_ref, b_ref, h_ref, hb_ref,
                *, alpha, tiles_p):
    is_p = pl.program_id(0) < tiles_p
    x = jnp.where(is_p, xp_ref[...], xs_ref[...])
    o_fox = jnp.where(is_p, ofp_ref[...], ofs_ref[...].astype(BF16))
    o_ret = jnp.where(is_p, orp_ref[...], ors_ref[...].astype(BF16))
    mixed = jnp.dot(o_fox, w1_ref[...], preferred_element_type=F32)
    mixed = mixed + jnp.dot(o_ret, w2_ref[...], preferred_element_type=F32)
    h = _layer_norm(alpha * x + mixed, g_ref[...], b_ref[...])
    h_ref[...] = h
    hb_ref[...] = h.astype(BF16)


def _wo_ln(x_p, x_s, of_p, of_s, or_p, or_s, w1, w2, g, b, alpha, tm):
    d = x_p.shape[1]
    tiles_p, tiles_s = x_p.shape[0] // tm, x_s.shape[0] // tm
    n = x_p.shape[0] + x_s.shape[0]
    row = lambda w: pl.BlockSpec((tm, w), lambda i: (i, 0))
    full = lambda a: pl.BlockSpec(a.shape, lambda i: (0, 0))
    return pl.pallas_call(
        functools.partial(_wo_ln_body, alpha=alpha, tiles_p=tiles_p),
        grid=(tiles_p + tiles_s,),
        in_specs=[*_group_specs(tm, d, tiles_p), *_group_specs(tm, of_p.shape[1], tiles_p),
                  *_group_specs(tm, or_p.shape[1], tiles_p), full(w1), full(w2), full(g), full(b)],
        out_specs=[row(d), row(d)],
        out_shape=[jax.ShapeDtypeStruct((n, d), F32), jax.ShapeDtypeStruct((n, d), BF16)],
        compiler_params=_params("arbitrary"),
        name="wo_ln1",
    )(x_p, x_s, of_p, of_s, or_p, or_s, w1, w2, g, b)


def _top_desc(s, count, n_masked, one_per_round):
    n = s.shape[0]

    vals, work = [], s
    if one_per_round:
        iota = lax.broadcasted_iota(jnp.int32, s.shape, 0)
        for _ in range(count):
            mj = jnp.max(work, axis=0, keepdims=True)
            vals.append(mj)
            first = jnp.min(jnp.where(work == mj, iota, n), axis=0, keepdims=True)
            work = jnp.where(iota == first, NEG_INF, work)
        return vals, jnp.zeros((1, s.shape[1]), F32)
    for _ in range(count):
        mj = jnp.max(work, axis=0, keepdims=True)
        vals.append(mj)
        work = jnp.where(work == mj, NEG_INF, work)
    removed = jnp.sum(jnp.where(work == NEG_INF, 1.0, 0.0), axis=0, keepdims=True)
    return vals, jnp.abs(removed - float(count + n_masked))


_CAND = [(a, b) for a in range(PEER_TOPK + 1) for b in range(PEER_TOPK + 1) if (a + 1) * (b + 1) <= PEER_TOPK + 1]


def _peer_score_body(hb_ref, wq_ref, keys_ref, t1_ref, e1_ref, e2_ref, q_sc):
    tm = hb_ref.shape[0]
    q_sc[...] = jnp.dot(hb_ref[...], wq_ref[...], preferred_element_type=F32).astype(BF16)

    def head(h, one_per_round):
        s = []
        for c in range(2):
            col = (2 * h + c) * PEER_HALF
            s.append(lax.dot_general(keys_ref[h, c], q_sc[:, col:col + PEER_HALF], _NT,
                                     preferred_element_type=F32))
        v1, bad1 = _top_desc(s[0], PEER_TOPK + 1, 0, one_per_round)
        v2, bad2 = _top_desc(s[1], PEER_TOPK + 1, 0, one_per_round)
        rows = [v1[a] + v2[b] for a, b in _CAND]
        pad = (-len(rows)) % 8
        cand = jnp.concatenate(rows + [jnp.full((pad, tm), NEG_INF, F32)], axis=0)
        top, bad3 = _top_desc(cand, PEER_TOPK + 1, pad, one_per_round)
        z = jnp.ones((1, tm), F32)
        for j in range(1, PEER_TOPK):
            z = z + jnp.exp(top[j] - top[0])
        theta = 0.5 * (top[PEER_TOPK - 1] + top[PEER_TOPK])
        c1 = s[0] - v1[0]
        t1_ref[h] = jnp.exp((theta - top[0]) - c1)
        e1_ref[h] = jnp.exp(c1 - jnp.log(z))
        e2_ref[h] = jnp.exp(s[1] - v2[0])
        return jnp.max(bad1 + bad2 + bad3)

    bad = [head(h, False) for h in range(PEER_HEADS)]
    for h in range(PEER_HEADS):
        @pl.when(bad[h] > 0.0)
        def _():
            head(h, True)


def _peer_scores(hb, wq, keys, tm):
    n, d = hb.shape
    return pl.pallas_call(
        _peer_score_body,
        grid=(n // tm,),
        in_specs=[pl.BlockSpec((tm, d), lambda i: (i, 0)), pl.BlockSpec(wq.shape, lambda i: (0, 0)),
                  pl.BlockSpec(keys.shape, lambda i: (0, 0, 0, 0))],
        out_specs=[pl.BlockSpec((PEER_HEADS, PEER_KEYS, tm), lambda i: (0, 0, i))] * 3,
        out_shape=[jax.ShapeDtypeStruct((PEER_HEADS, PEER_KEYS, n), F32)] * 3,
        scratch_shapes=[pltpu.VMEM((tm, wq.shape[1]), BF16)],
        compiler_params=_params("parallel"),
        name="peer_scores",
    )(hb, wq, keys)


PEER_SUB = 256


PEER_EB = 2 * PEER_SUB


def _peer_dense_body(hb_ref, u_ref, v0_ref, vp_ref, t1_ref, e1_ref, e2_ref, o_ref, act_sc, w_sc, acc_sc):
    j = pl.program_id(1)
    n_steps = pl.num_programs(1) - 1
    tm = hb_ref.shape[0]
    n_i1 = PEER_SUB // PEER_KEYS

    @pl.when(j == 0)
    def _():
        o_ref[...] = jnp.zeros(o_ref.shape, F32)
        acc_sc[...] = jnp.zeros(acc_sc.shape, F32)
        w_sc[1] = jnp.zeros(w_sc.shape[1:], BF16)

    d_model = o_ref.shape[1]
    n_tiles = tm // 128
    col_chunks = [slice(q * 256, (q + 1) * 256) for q in range(d_model // 256)]

    def product(slot, v_ref, cols):
        dst = acc_sc if slot == 1 else o_ref
        dst[:, cols] += jnp.dot(w_sc[slot], v_ref[:, cols].astype(BF16), preferred_element_type=F32)

    def build_w(r, prev_slot, prev_v):
        act = lax.dot_general(hb_ref[...], u_ref[r * PEER_SUB:(r + 1) * PEER_SUB, :].astype(BF16), _NT,
                              preferred_element_type=F32)
        act_sc[r] = 0.5 * act * (1.0 + lax.erf(act * (2.0 ** -0.5)))
        i1_0 = (2 * j + r) * n_i1
        t1_rows = [[t1_ref[h, pl.ds(i1_0 + k, 1), :] for k in range(n_i1)] for h in range(PEER_HEADS)]
        e1_rows = [[e1_ref[h, pl.ds(i1_0 + k, 1), :] for k in range(n_i1)] for h in range(PEER_HEADS)]
        pending_chunks = list(col_chunks)
        for c in range(n_tiles):
            lanes = slice(c * 128, (c + 1) * 128)
            for k in range(n_i1):
                gate = None
                for h in range(PEER_HEADS):
                    e2 = e2_ref[h, :, lanes]
                    g = jnp.where(e2 >= t1_rows[h][k][:, lanes], e2, 0.0) * e1_rows[h][k][:, lanes]
                    gate = g if gate is None else gate + g
                cols = slice(k * PEER_KEYS, (k + 1) * PEER_KEYS)
                w_sc[r, lanes, cols] = (act_sc[r, lanes, cols] * gate.T).astype(BF16)
            n_now = -(-len(pending_chunks) // (n_tiles - c))
            for _ in range(n_now):
                product(prev_slot, prev_v, pending_chunks.pop(0))

    @pl.when(j < n_steps)
    def _main():
        build_w(0, 1, vp_ref)
        build_w(1, 0, v0_ref)

    @pl.when(j == n_steps)
    def _drain():
        for cols in col_chunks:
            product(1, vp_ref, cols)
        o_ref[...] += acc_sc[...]


def _peer_dense(hb, u, v, t1, e1, e2, tm):
    n, d = hb.shape
    n_steps = u.shape[0] // PEER_EB
    last = n_steps - 1
    per_tile = lambda shape, imap: pl.BlockSpec(shape, imap, pipeline_mode=pl.Buffered(1))
    gate_spec = lambda: per_tile((PEER_HEADS, PEER_KEYS, tm), lambda i, j: (0, 0, i))
    return pl.pallas_call(
        _peer_dense_body,
        grid=(n // tm, n_steps + 1),
        in_specs=[
            per_tile((tm, d), lambda i, j: (i, 0)),
            pl.BlockSpec((PEER_EB, d), lambda i, j: (jnp.minimum(j, last), 0)),
            pl.BlockSpec((PEER_SUB, d), lambda i, j: (2 * jnp.minimum(j, last), 0)),
            pl.BlockSpec((PEER_SUB, d), lambda i, j: (jnp.maximum(2 * j - 1, 0), 0)),
            gate_spec(), gate_spec(), gate_spec(),
        ],
        out_specs=pl.BlockSpec((tm, d), lambda i, j: (i, 0)),
        out_shape=jax.ShapeDtypeStruct((n, d), F32),
        scratch_shapes=[pltpu.VMEM((2, tm, PEER_SUB), F32), pltpu.VMEM((2, tm, PEER_SUB), BF16),
                        pltpu.VMEM((tm, d), F32)],
        compiler_params=_params("parallel", "arbitrary"),
        name="peer_dense",
    )(hb, u, v, v, t1, e1, e2)


def _out_body(h_ref, po_ref, pp_ref, ps_ref, g_ref, b_ref, wg_ref, bg_ref, we_ref, yp_ref, ys_ref,
              *, alpha, tiles_p):
    i = pl.program_id(0)
    h2 = _layer_norm(alpha * h_ref[...] + po_ref[...], g_ref[...], b_ref[...])
    lin = jnp.dot(h2.astype(BF16), wg_ref[...], preferred_element_type=F32) + bg_ref[...]
    p_emb = jnp.where(i < tiles_p, pp_ref[...], ps_ref[...])
    emb = jnp.dot(p_emb.astype(BF16), we_ref[...], preferred_element_type=F32)
    y = h2 + emb / (1.0 + jnp.exp(-lin))

    @pl.when(i < tiles_p)
    def _():
        yp_ref[...] = y

    @pl.when(i >= tiles_p)
    def _():
        ys_ref[...] = y


def _out_stage(h1, peer_out, p_p, p_s, g, b, wg, bg, we, alpha, tm):
    n, d = h1.shape
    tiles_p, tiles_s = p_p.shape[0] // tm, p_s.shape[0] // tm
    assert (tiles_p + tiles_s) * tm == n
    row = lambda w: pl.BlockSpec((tm, w), lambda i: (i, 0))
    full = lambda a: pl.BlockSpec(a.shape, lambda i: (0, 0))
    return pl.pallas_call(
        functools.partial(_out_body, alpha=alpha, tiles_p=tiles_p),
        grid=(tiles_p + tiles_s,),
        in_specs=[row(d), row(d), *_group_specs(tm, p_p.shape[1], tiles_p),
                  full(g), full(b), full(wg), full(bg), full(we)],
        out_specs=list(_group_specs(tm, d, tiles_p)),
        out_shape=[jax.ShapeDtypeStruct((tiles_p * tm, d), F32), jax.ShapeDtypeStruct((tiles_s * tm, d), F32)],
        compiler_params=_params("arbitrary"),
        name="ln2_gate_out",
    )(h1, peer_out, p_p, p_s, g, b, wg, bg, we)


def kernel(x_prompt, x_sample, cache_k, cache_v, cache_logf, state_ret, page_table, p_prompt, p_sample, w_in, b_f,
           gn_g, w_o, ln1_g, ln1_b, w_pq, peer_keys, peer_u, peer_v, ln2_g, ln2_b, w_pg, b_pg, w_pe):
    depth = w_in.shape[0]
    assert depth == 1, "one layer"
    batch, seq, d_model = x_prompt.shape
    db, t_dec, _ = x_sample.shape
    n_pool = cache_k.shape[1]
    n_fox = cache_k.shape[3]
    n_heads = d_model // HEAD_DIM
    n_ret = n_heads - n_fox
    w_fox = n_fox * HEAD_DIM
    w_ret = n_ret * HEAD_DIM
    n_pages = page_table.shape[1]
    past_len = n_pages * PAGE_SIZE
    np_tok = batch * seq
    ns_tok = db * t_dec
    n_tok = np_tok + ns_tok
    alpha = (2.0 * depth) ** 0.25
    assert cache_k.shape[2] == PAGE_SIZE and cache_k.shape[4] == HEAD_DIM and t_dec % 8 == 0

    tm_big = _pick(n_tok, (768, 512, 384, 256, 128))
    tm_mid = _pick(math.gcd(np_tok, ns_tok), (256, 128))

    wi = w_in[0]
    f0 = 3 * w_fox
    w_main = jnp.concatenate([wi[:, :f0], wi[:, f0 + n_fox:]], axis=1).astype(BF16)
    wf = wi[:, f0:f0 + n_fox].astype(BF16)
    wo1 = w_o[0, :w_fox].astype(BF16)
    wo2 = w_o[0, w_fox:].astype(BF16)
    wq = w_pq[0].astype(BF16)
    keys = peer_keys[0].astype(BF16)
    u_b = peer_u[0]
    v_b = peer_v[0]
    wg = w_pg[0].astype(BF16)
    we = w_pe[0].astype(BF16)
    row = lambda a: a.reshape(1, -1)

    x_p = x_prompt.reshape(np_tok, d_model)
    x_s = x_sample.reshape(ns_tok, d_model)
    b_col = b_f[0].reshape(n_fox, 1)

    qf_p, kf_p, vf_p, zr_p, lf_p = _in_proj(x_p, w_main, wf, b_col, w_fox)
    qf_s, kf_s, vf_s, zr_s, lf_s = _in_proj(x_s, w_main, wf, b_col, w_fox)

    c_t = _segment_cumsum(lf_p, seq)
    o_fox_p = _fox_prompt(qf_p, kf_p, vf_p, c_t, batch, seq, n_fox)

    cn_t = _segment_cumsum(jnp.pad(lf_s, ((0, 0), (0, (-ns_tok) % 128))), t_dec)[:, :ns_tok]
    cn = cn_t.reshape(n_fox, db, t_dec).transpose(1, 0, 2)
    cn_pad = jnp.pad(cn, ((0, 0), (0, 0), (0, PAGE_SIZE - t_dec)), constant_values=BIG)
    page_et = _page_suffix(cache_logf[0].reshape(n_pool, PAGE_SIZE * n_fox))
    o_fox_s = _fox_sample(qf_s, kf_s, vf_s, cn_pad, cache_k, cache_v, page_et, page_table, t_dec, n_fox)

    gn_row = row(gn_g[0])
    cos_p, sin_p = _rotary_tables(jnp.arange(seq, dtype=F32))
    cos_s, sin_s = _rotary_tables(past_len + jnp.arange(t_dec, dtype=F32))
    zeros_state = jnp.zeros((batch, n_ret, HEAD_DIM, HEAD_DIM), F32)
    o_ret_p, s_p = _retention(zr_p, 0, 0, n_ret, batch, seq, RET_CHUNK, zeros_state, cos_p, sin_p,
                              _retention_tables(n_ret, RET_CHUNK), gn_row)
    o_ret_s, s_s = _retention(zr_s, 0, 0, n_ret, db, t_dec, t_dec, state_ret[0], cos_s, sin_s,
                              _retention_tables(n_ret, t_dec), gn_row)

    h1, h1b = _wo_ln(x_p, x_s, o_fox_p, o_fox_s, o_ret_p, o_ret_s, wo1, wo2, row(ln1_g[0]), row(ln1_b[0]),
                     alpha, tm_mid)
    t1, e1, e2 = _peer_scores(h1b, wq, keys, _pick(n_tok, (128,)))
    peer_out = _peer_dense(h1b, u_b, v_b, t1, e1, e2, tm_big)
    y_p, y_s = _out_stage(h1, peer_out, p_prompt[0].reshape(np_tok, -1), p_sample[0].reshape(ns_tok, -1),
                          row(ln2_g[0]), row(ln2_b[0]), wg, row(b_pg[0]), we, alpha, tm_mid)

    heads_p = lambda a: a.reshape(1, batch, seq, n_fox, HEAD_DIM)
    heads_s = lambda a: a.reshape(1, db, t_dec, n_fox, HEAD_DIM)
    return (y_p.reshape(batch, seq, d_model), y_s.reshape(db, t_dec, d_model),
            heads_p(kf_p), heads_p(vf_p), lf_p.T.reshape(1, batch, seq, n_fox), s_p[None],
            heads_s(kf_s), heads_s(vf_s), lf_s.T.reshape(1, db, t_dec, n_fox), s_s[None])
```
